```python
import jax, jax.numpy as jnp
from jax import lax
import numpy as np

D_MODEL = 2048
BATCH = 8
SEQ = 4096
DEPTH = 4
DEC_BATCH = 8
DEC_SEQ = 2048
PAST_LEN = 128

MIX_WIDTH = D_MODEL
ATTN_WIDTH = MIX_WIDTH // 2
POOL_WIDTH = MIX_WIDTH - ATTN_WIDTH
N_HEADS = 8
QK_NOPE_DIM = 128
QK_ROPE_DIM = 64
V_HEAD_DIM = ATTN_WIDTH // N_HEADS
Q_LORA_RANK = 512
KV_LORA_RANK = 512
ROPE_BASE = 10000.0
Q_BLOCK = 128
POOL_WINDOWS = (2, 4, 8, 16)
N_POOL_GROUPS = len(POOL_WINDOWS)
POOL_GROUP_WIDTH = POOL_WIDTH // N_POOL_GROUPS
IN_WIDTH = Q_LORA_RANK + KV_LORA_RANK + QK_ROPE_DIM + POOL_WIDTH
D_FF = 5632
N_FFN = 2
N_MOD = 9
NORM_EPS = 1e-6

kernel_name = "hymba_mla_pool_macaron_adaln_encoder"


def rms_norm(x, gain):
    xf = x.astype(jnp.float32)
    y = xf * lax.rsqrt(jnp.mean(xf * xf, axis=-1, keepdims=True) + NORM_EPS)
    return (y * gain.astype(jnp.float32)).astype(x.dtype)


def rope_tables(seq):
    inv = 1.0 / (ROPE_BASE ** (jnp.arange(0, QK_ROPE_DIM, 2, dtype=jnp.float32) / QK_ROPE_DIM))
    ang = jnp.arange(seq, dtype=jnp.float32)[:, None] * inv[None, :]
    return jnp.cos(ang), jnp.sin(ang)


def apply_rope(x, cos, sin):
    x1, x2 = jnp.split(x.astype(jnp.float32), 2, axis=-1)
    return jnp.concatenate([x1 * cos - x2 * sin, x1 * sin + x2 * cos], axis=-1).astype(x.dtype)


def swiglu(h, w_gate, w_up, w_down):
    return (jax.nn.silu(h @ w_gate) * (h @ w_up)) @ w_down


def mla(q_lat, kv_lat, k_rope, q_norm_g, kv_norm_g, w_uq, w_ukv, cos, sin):
    B, S, _ = q_lat.shape
    q = (rms_norm(q_lat, q_norm_g) @ w_uq).reshape(B, S, N_HEADS, QK_NOPE_DIM + QK_ROPE_DIM)
    q_nope = q[..., :QK_NOPE_DIM]
    q_rope = apply_rope(q[..., QK_NOPE_DIM:], cos[:, None, :], sin[:, None, :])
    k_rope = apply_rope(k_rope, cos, sin)
    kv = (rms_norm(kv_lat, kv_norm_g) @ w_ukv).reshape(B, S, N_HEADS, QK_NOPE_DIM + V_HEAD_DIM)
    k_nope = kv[..., :QK_NOPE_DIM]
    v = kv[..., QK_NOPE_DIM:]
    scale = (QK_NOPE_DIM + QK_ROPE_DIM) ** -0.5
    nb = S // Q_BLOCK

    def to_blocks(t):
        return jnp.moveaxis(t.reshape(B, nb, Q_BLOCK, *t.shape[2:]), 1, 0)

    def block(qs):
        qn, qr = qs
        s = (jnp.einsum('bqhd,bkhd->bhqk', qn, k_nope, preferred_element_type=jnp.float32)
             + jnp.einsum('bqhr,bkr->bhqk', qr, k_rope, preferred_element_type=jnp.float32)) * scale
        p = jax.nn.softmax(s, axis=-1).astype(v.dtype)
        return jnp.einsum('bhqk,bkhd->bqhd', p, v)

    o = lax.map(block, (to_blocks(q_nope), to_blocks(q_rope)))
    return jnp.moveaxis(o, 0, 1).reshape(B, S, N_HEADS * V_HEAD_DIM)


def multiscale_pool(u, w_pool, pool_scale):
    B, S, _ = u.shape
    uf = u.reshape(B, S, N_POOL_GROUPS, POOL_GROUP_WIDTH).astype(jnp.float32)
    cs = jnp.pad(jnp.cumsum(uf, axis=1), ((0, 0), (1, 0), (0, 0), (0, 0)))
    t = jnp.arange(S)
    outs = []
    for g, w in enumerate(POOL_WINDOWS):
        lo = jnp.clip(t - w // 2 + 1, 0, S)
        hi = jnp.clip(t + w // 2 + 1, 0, S)
        csg = cs[:, :, g]
        win_sum = csg[:, hi] - csg[:, lo]
        count = (hi - lo).astype(jnp.float32)[None, :, None]
        outs.append(win_sum / count - uf[:, :, g])
    pooled = jnp.stack(outs, axis=2).astype(u.dtype)
    y = jnp.einsum('bsgc,gcd->bsgd', pooled, w_pool).reshape(B, S, POOL_WIDTH)
    return y * pool_scale


def trunk(x, c, norm_g, w_mod, b_mod, ffn_w_gate, ffn_w_up, ffn_w_down, w_in,
          q_norm_g, kv_norm_g, w_uq, w_ukv, attn_out_g, w_pool, pool_scale, w_out,
          w_mod_final, b_mod_final, final_g):
    B, S, _ = x.shape
    cos, sin = rope_tables(S)
    cond = jax.nn.silu(c)
    split_at = [Q_LORA_RANK, Q_LORA_RANK + KV_LORA_RANK, Q_LORA_RANK + KV_LORA_RANK + QK_ROPE_DIM]
    for l in range(DEPTH):
        mod = (cond @ w_mod[l] + b_mod[l]).astype(x.dtype)[:, None, :]
        sh1, sc1, g1, sh2, sc2, g2, sh3, sc3, g3 = jnp.split(mod, N_MOD, axis=-1)
        h = rms_norm(x, norm_g[l, 0]) * (1.0 + sc1) + sh1
        x = x + 0.5 * g1 * swiglu(h, ffn_w_gate[l, 0], ffn_w_up[l, 0], ffn_w_down[l, 0])
        h = rms_norm(x, norm_g[l, 1]) * (1.0 + sc2) + sh2
        z = h @ w_in[l]
        q_lat, kv_lat, k_r, u = jnp.split(z, split_at, axis=-1)
        a = rms_norm(mla(q_lat, kv_lat, k_r, q_norm_g[l], kv_norm_g[l], w_uq[l], w_ukv[l], cos, sin),
                     attn_out_g[l])
        p = multiscale_pool(u, w_pool[l], pool_scale[l])
        x = x + g2 * (jnp.concatenate([a, p], axis=-1) @ w_out[l])
        h = rms_norm(x, norm_g[l, 2]) * (1.0 + sc3) + sh3
        x = x + 0.5 * g3 * swiglu(h, ffn_w_gate[l, 1], ffn_w_up[l, 1], ffn_w_down[l, 1])
    fm = (cond @ w_mod_final + b_mod_final).astype(x.dtype)[:, None, :]
    shf, scf = jnp.split(fm, 2, axis=-1)
    return rms_norm(x, final_g) * (1.0 + scf) + shf


def setup_inputs(seed: int = 0) -> dict:
    key = jax.random.key(seed)
    ks = jax.random.split(key, 24)
    f32 = jnp.float32

    def w(k, shape, fan_in, mult=1.0):
        return jax.random.normal(k, shape, f32) * (mult * fan_in ** -0.5)

    def gain(k, shape):
        return 1.0 + 0.02 * jax.random.normal(k, shape, f32)

    return {
        "x_prompt": jax.random.normal(ks[0], (BATCH, SEQ, D_MODEL), f32),
        "x_sample": jax.random.normal(ks[1], (DEC_BATCH, DEC_SEQ, D_MODEL), f32),
        "c_prompt": jax.random.normal(ks[2], (BATCH, D_MODEL), f32),
        "c_sample": jax.random.normal(ks[3], (DEC_BATCH, D_MODEL), f32),
        "norm_g": gain(ks[4], (DEPTH, 3, D_MODEL)),
        "w_mod": w(ks[5], (DEPTH, D_MODEL, N_MOD * D_MODEL), D_MODEL, 0.5),
        "b_mod": 0.01 * jax.random.normal(ks[6], (DEPTH, N_MOD * D_MODEL), f32),
        "ffn_w_gate": w(ks[7], (DEPTH, N_FFN, D_MODEL, D_FF), D_MODEL),
        "ffn_w_up": w(ks[8], (DEPTH, N_FFN, D_MODEL, D_FF), D_MODEL),
        "ffn_w_down": w(ks[9], (DEPTH, N_FFN, D_FF, D_MODEL), D_FF),
        "w_in": w(ks[10], (DEPTH, D_MODEL, IN_WIDTH), D_MODEL),
        "q_norm_g": gain(ks[11], (DEPTH, Q_LORA_RANK)),
        "kv_norm_g": gain(ks[12], (DEPTH, KV_LORA_RANK)),
        "w_uq": w(ks[13], (DEPTH, Q_LORA_RANK, N_HEADS * (QK_NOPE_DIM + QK_ROPE_DIM)), Q_LORA_RANK),
        "w_ukv": w(ks[14], (DEPTH, KV_LORA_RANK, N_HEADS * (QK_NOPE_DIM + V_HEAD_DIM)), KV_LORA_RANK),
        "attn_out_g": gain(ks[15], (DEPTH, ATTN_WIDTH)),
        "w_pool": w(ks[16], (DEPTH, N_POOL_GROUPS, POOL_GROUP_WIDTH, POOL_GROUP_WIDTH), POOL_GROUP_WIDTH),
        "pool_scale": 1.0 + 0.1 * jax.random.normal(ks[17], (DEPTH, POOL_WIDTH), f32),
        "w_out": w(ks[18], (DEPTH, MIX_WIDTH, D_MODEL), MIX_WIDTH),
        "w_mod_final": w(ks[19], (D_MODEL, 2 * D_MODEL), D_MODEL, 0.5),
        "b_mod_final": 0.01 * jax.random.normal(ks[20], (2 * D_MODEL,), f32),
        "final_g": gain(ks[21], (D_MODEL,)),
    }


def reference(x_prompt, x_sample, c_prompt, c_sample, norm_g, w_mod, b_mod, ffn_w_gate, ffn_w_up,
              ffn_w_down, w_in, q_norm_g, kv_norm_g, w_uq, w_ukv, attn_out_g, w_pool, pool_scale,
              w_out, w_mod_final, b_mod_final, final_g):
    y_prompt = trunk(x_prompt, c_prompt, norm_g, w_mod, b_mod, ffn_w_gate, ffn_w_up, ffn_w_down, w_in,
                     q_norm_g, kv_norm_g, w_uq, w_ukv, attn_out_g, w_pool, pool_scale, w_out,
                     w_mod_final, b_mod_final, final_g)
    y_sample = trunk(x_sample, c_sample, norm_g, w_mod, b_mod, ffn_w_gate, ffn_w_up, ffn_w_down, w_in,
                     q_norm_g, kv_norm_g, w_uq, w_ukv, attn_out_g, w_pool, pool_scale, w_out,
                     w_mod_final, b_mod_final, final_g)
    return (y_prompt, y_sample)
```

```python
import functools
import math

import jax
import jax.numpy as jnp
from jax import lax
from jax.experimental import pallas as pl
from jax.experimental.pallas import tpu as pltpu

F32 = jnp.float32
BF16 = jnp.bfloat16

N_HEADS = 8
QK_NOPE_DIM = 128
QK_ROPE_DIM = 64
V_HEAD_DIM = 128
Q_LORA_RANK = 512
KV_LORA_RANK = 512
ROPE_BASE = 10000.0
POOL_WINDOWS = (2, 4, 8, 16)
POOL_GROUP_WIDTH = 256
POOL_WIDTH = POOL_GROUP_WIDTH * len(POOL_WINDOWS)
ATTN_WIDTH = N_HEADS * V_HEAD_DIM
N_MOD = 9
NORM_EPS = 1e-6
HEAD_QK = QK_NOPE_DIM + 2 * QK_ROPE_DIM
HALO = 8
LOG2E = math.log2(math.e)

VMEM_LIMIT_CAP = 60 * 1024 * 1024


def _vmem_limit(block_bytes, scratch_bytes, temp_bytes):
    return int(min(VMEM_LIMIT_CAP, 2 * block_bytes + scratch_bytes + temp_bytes + (4 << 20)))


def _pick_tile(n, preferred):
    for t in preferred:
        if n % t == 0:
            return t
    return n


def _rms(x, gain):
    ms = jnp.mean(x * x, axis=-1, keepdims=True)
    return x * lax.rsqrt(ms + NORM_EPS) * gain


def _mod_kernel(c_ref, w_ref, b_ref, o_ref):
    c = c_ref[...]
    cond = (c * jax.nn.sigmoid(c)).astype(BF16)
    o_ref[...] = jnp.dot(cond, w_ref[...].astype(BF16), preferred_element_type=F32) + b_ref[...]


def _modulation(c, w, b):
    L, D, N = w.shape
    R = c.shape[0]
    tn = _pick_tile(N, (1024, 512, 256, 128))
    return pl.pallas_call(
        _mod_kernel,
        grid=(L, N // tn),
        in_specs=[
            pl.BlockSpec((R, D), lambda l, j: (0, 0)),
            pl.BlockSpec((None, D, tn), lambda l, j: (l, 0, j)),
            pl.BlockSpec((None, 1, tn), lambda l, j: (l, 0, j)),
        ],
        out_specs=pl.BlockSpec((None, R, tn), lambda l, j: (l, 0, j)),
        out_shape=jax.ShapeDtypeStruct((L, R, N), F32),
        compiler_params=pltpu.CompilerParams(
            dimension_semantics=("parallel", "parallel"),
            vmem_limit_bytes=_vmem_limit(D * tn * 4 + R * D * 4 + R * tn * 4, 0, D * tn * 2),
        ),
        name="adaln_mod",
    )(c, w, b.reshape(L, 1, N))


def _ffn_kernel(x_ref, mod_ref, g_ref, wg_ref, wu_ref, wd_ref, o_ref, h_ref, *, mod_base):
    j = pl.program_id(1)

    @pl.when(j == 0)
    def _():
        shift = mod_ref[0, mod_base:mod_base + 1, :]
        scale = mod_ref[0, mod_base + 1:mod_base + 2, :]
        h_ref[...] = (_rms(x_ref[...], g_ref[...]) * (1.0 + scale) + shift).astype(BF16)
        o_ref[...] = jnp.zeros_like(o_ref)

    h = h_ref[...]
    gate = jnp.dot(h, wg_ref[...], preferred_element_type=F32)
    up = jnp.dot(h, wu_ref[...], preferred_element_type=F32)
    act = (gate * jax.nn.sigmoid(gate) * up).astype(BF16)
    o_ref[...] += jnp.dot(act, wd_ref[...], preferred_element_type=F32)

    @pl.when(j == pl.num_programs(1) - 1)
    def _():
        res_gate = mod_ref[0, mod_base + 2:mod_base + 3, :]
        o_ref[...] = x_ref[...] + 0.5 * res_gate * o_ref[...]


def _ffn(x, mod, gain, wg, wu, wd, layer, half, mod_base, seq_len):
    T, D = x.shape
    F = wg.shape[-1]
    tm = _pick_tile(seq_len, (512, 256, 128))
    tf = _pick_tile(F, (512, 256, 128))
    tps = seq_len // tm
    return pl.pallas_call(
        functools.partial(_ffn_kernel, mod_base=mod_base),
        grid=(T // tm, F // tf),
        in_specs=[
            pl.BlockSpec((tm, D), lambda i, j: (i, 0)),
            pl.BlockSpec((1, N_MOD, D), lambda i, j: (i // tps, 0, 0)),
            pl.BlockSpec((1, D), lambda i, j: (0, 0)),
            pl.BlockSpec((None, None, D, tf), lambda i, j: (layer, half, 0, j)),
            pl.BlockSpec((None, None, D, tf), lambda i, j: (layer, half, 0, j)),
            pl.BlockSpec((None, None, tf, D), lambda i, j: (layer, half, j, 0)),
        ],
        out_specs=pl.BlockSpec((tm, D), lambda i, j: (i, 0)),
        out_shape=jax.ShapeDtypeStruct((T, D), F32),
        scratch_shapes=[pltpu.VMEM((tm, D), BF16)],
        compiler_params=pltpu.CompilerParams(
            dimension_semantics=("parallel", "arbitrary"),
            vmem_limit_bytes=_vmem_limit(
                2 * tm * D * 4 + 3 * D * tf * 2 + 16 * D * 4, tm * D * 2, 3 * tm * tf * 4 + tm * D * 4),
        ),
        name="ffn",
    )(x, mod, gain, wg, wu, wd)


def _mix_in_kernel(x_ref, mod_ref, g_ref, win_ref, qg_ref, kvg_ref, wuq_ref, wukv_ref, qtab_ref, ktab_ref,
                   q_ref, k_ref, v_ref, u_ref):
    shift = mod_ref[0, 3:4, :]
    scale = mod_ref[0, 4:5, :]
    h = (_rms(x_ref[...], g_ref[...]) * (1.0 + scale) + shift).astype(BF16)
    z = jnp.dot(h, win_ref[...], preferred_element_type=F32)
    c_kv = Q_LORA_RANK
    c_kr = c_kv + KV_LORA_RANK
    c_u = c_kr + 2 * QK_ROPE_DIM
    u_ref[...] = z[:, c_u:]

    qn = _rms(z[:, :c_kv], qg_ref[...]).astype(BF16)
    kvn = _rms(z[:, c_kv:c_kr], kvg_ref[...]).astype(BF16)
    q = jnp.dot(qn, wuq_ref[...], preferred_element_type=F32)
    kv = jnp.dot(kvn, wukv_ref[...], preferred_element_type=F32)

    t = z[:, c_kr:c_u] * ktab_ref[...]
    k_rot = (t + pltpu.roll(t, QK_ROPE_DIM, axis=1)).astype(BF16)
    qtab = qtab_ref[...]
    for hd in range(N_HEADS):
        c0 = hd * HEAD_QK
        q_ref[:, c0:c0 + HEAD_QK] = (q[:, c0:c0 + HEAD_QK] * qtab).astype(BF16)
        k_ref[:, c0:c0 + QK_NOPE_DIM] = kv[:, c0:c0 + QK_NOPE_DIM].astype(BF16)
        k_ref[:, c0 + QK_NOPE_DIM:c0 + HEAD_QK] = k_rot
        v_ref[:, hd * V_HEAD_DIM:(hd + 1) * V_HEAD_DIM] = kv[:, c0 + QK_NOPE_DIM:c0 + HEAD_QK].astype(BF16)


def _mix_in(x, mod, gain, win, qg, kvg, wuq, wukv, qtab, ktab, layer, seq_len):
    T, D = x.shape
    n_in = win.shape[-1]
    tm = _pick_tile(seq_len, (512, 256, 128))
    tps = seq_len // tm
    qk_w = N_HEADS * HEAD_QK
    whole = lambda shape: pl.BlockSpec(shape, lambda i: (0,) * len(shape))
    layer_mat = lambda r, c: pl.BlockSpec((None, r, c), lambda i: (layer, 0, 0))
    block_bytes = (tm * D * 4 + D * n_in * 2 + Q_LORA_RANK * qk_w * 2 + KV_LORA_RANK * qk_w * 2
                   + tm * (HEAD_QK + 2 * QK_ROPE_DIM) * 4 + 2 * tm * qk_w * 2 + tm * ATTN_WIDTH * 2
                   + tm * POOL_WIDTH * 4 + 16 * D * 4)
    return pl.pallas_call(
        _mix_in_kernel,
        grid=(T // tm,),
        in_specs=[
            pl.BlockSpec((tm, D), lambda i: (i, 0)),
            pl.BlockSpec((1, N_MOD, D), lambda i: (i // tps, 0, 0)),
            whole((1, D)),
            layer_mat(D, n_in),
            whole((1, Q_LORA_RANK)),
            whole((1, KV_LORA_RANK)),
            layer_mat(Q_LORA_RANK, qk_w),
            layer_mat(KV_LORA_RANK, qk_w),
            pl.BlockSpec((tm, HEAD_QK), lambda i: (i % tps, 0)),
            pl.BlockSpec((tm, 2 * QK_ROPE_DIM), lambda i: (i % tps, 0)),
        ],
        out_specs=[
            pl.BlockSpec((tm, qk_w), lambda i: (i, 0)),
            pl.BlockSpec((tm, qk_w), lambda i: (i, 0)),
            pl.BlockSpec((tm, ATTN_WIDTH), lambda i: (i, 0)),
            pl.BlockSpec((tm, POOL_WIDTH), lambda i: (i, 0)),
        ],
        out_shape=[
            jax.ShapeDtypeStruct((T, qk_w), BF16),
            jax.ShapeDtypeStruct((T, qk_w), BF16),
            jax.ShapeDtypeStruct((T, ATTN_WIDTH), BF16),
            jax.ShapeDtypeStruct((T, POOL_WIDTH), F32),
        ],
        compiler_params=pltpu.CompilerParams(
            dimension_semantics=("parallel",),
            vmem_limit_bytes=_vmem_limit(block_bytes, 0, tm * (n_in + 2 * qk_w) * 4 + tm * D * 4),
        ),
        name="mix_in",
    )(x, mod, gain, win, qg, kvg, wuq, wukv, qtab, ktab)


def _attn_kernel(q_ref, k_ref, v_ref, o_ref, m_ref, l_ref, acc_ref, *, tk):
    q = q_ref[0]
    n_chunks = k_ref.shape[1] // tk
    m_ref[...] = jnp.full_like(m_ref, -jnp.inf)
    l_ref[...] = jnp.zeros_like(l_ref)
    acc_ref[...] = jnp.zeros_like(acc_ref)

    def chunk(c, carry):
        start = pl.multiple_of(c * tk, tk)
        k = k_ref[0, pl.ds(start, tk), :]
        v = v_ref[0, pl.ds(start, tk), :]
        s = lax.dot_general(q, k, (((1,), (1,)), ((), ())), preferred_element_type=F32)
        m_prev = m_ref[...]
        m_new = jnp.maximum(m_prev, jnp.max(s, axis=-1, keepdims=True))
        alpha = jnp.exp2(m_prev - m_new)
        p = jnp.exp2(s - m_new)
        l_ref[...] = alpha * l_ref[...] + jnp.sum(p, axis=-1, keepdims=True)
        acc_ref[...] = alpha * acc_ref[...] + jnp.dot(p.astype(BF16), v, preferred_element_type=F32)
        m_ref[...] = m_new
        return carry

    lax.fori_loop(0, n_chunks, chunk, 0)
    o_ref[0] = acc_ref[...] / l_ref[...]


def _attention(q, k, v):
    B, S, _ = q.shape
    tq = _pick_tile(S, (512, 256, 128))
    tk = _pick_tile(S, (512, 256, 128))
    block_bytes = tq * HEAD_QK * 2 + S * HEAD_QK * 2 + S * V_HEAD_DIM * 2 + tq * V_HEAD_DIM * 4
    return pl.pallas_call(
        functools.partial(_attn_kernel, tk=tk),
        grid=(B, N_HEADS, S // tq),
        in_specs=[
            pl.BlockSpec((1, tq, HEAD_QK), lambda b, h, i: (b, i, h)),
            pl.BlockSpec((1, S, HEAD_QK), lambda b, h, i: (b, 0, h)),
            pl.BlockSpec((1, S, V_HEAD_DIM), lambda b, h, i: (b, 0, h)),
        ],
        out_specs=pl.BlockSpec((1, tq, V_HEAD_DIM), lambda b, h, i: (b, i, h)),
        out_shape=jax.ShapeDtypeStruct((B, S, N_HEADS * V_HEAD_DIM), F32),
        scratch_shapes=[
            pltpu.VMEM((tq, 1), F32),
            pltpu.VMEM((tq, 1), F32),
            pltpu.VMEM((tq, V_HEAD_DIM), F32),
        ],
        compiler_params=pltpu.CompilerParams(
            dimension_semantics=("parallel", "parallel", "parallel"),
            vmem_limit_bytes=_vmem_limit(block_bytes, 3 * tq * 128 * 4, 4 * tq * tk * 4),
        ),
        name="attn",
    )(q, k, v)


def _mix_out_kernel(x_ref, mod_ref, ao_ref, aog_ref, u_ref, up_ref, un_ref, wpool_ref, ps_ref, wout_ref,
                    o_ref, ext_ref, cat_ref, *, tps, seq_len):
    tm = x_ref.shape[0]
    si = pl.program_id(0) % tps
    cat_ref[:, :ATTN_WIDTH] = _rms(ao_ref[...], aog_ref[...]).astype(BF16)

    u = u_ref[...]
    ext_ref[0:HALO, :] = jnp.where(si == 0, 0.0, up_ref[...])
    ext_ref[HALO:HALO + tm, :] = u
    ext_ref[HALO + tm:2 * HALO + tm, :] = jnp.where(si == tps - 1, 0.0, un_ref[...])
    pos = si * tm + lax.broadcasted_iota(jnp.int32, (tm, 1), 0)
    for g, w in enumerate(POOL_WINDOWS):
        c0 = g * POOL_GROUP_WIDTH
        cols = slice(c0, c0 + POOL_GROUP_WIDTH)
        first = -(w // 2 - 1)
        win_sum = ext_ref[pl.ds(HALO + first, tm), cols]
        for d in range(first + 1, w // 2 + 1):
            win_sum = win_sum + ext_ref[pl.ds(HALO + d, tm), cols]
        lo = jnp.maximum(pos + first, 0)
        hi = jnp.minimum(pos + w // 2 + 1, seq_len)
        pooled = win_sum / (hi - lo).astype(F32) - u[:, cols]
        y = jnp.dot(pooled.astype(BF16), wpool_ref[g], preferred_element_type=F32) * ps_ref[:, cols]
        cat_ref[:, ATTN_WIDTH + c0:ATTN_WIDTH + c0 + POOL_GROUP_WIDTH] = y.astype(BF16)

    out = jnp.dot(cat_ref[...], wout_ref[...], preferred_element_type=F32)
    o_ref[...] = x_ref[...] + mod_ref[0, 5:6, :] * out


def _mix_out(x, mod, ao, aog, u, wpool, ps, wout, layer, seq_len):
    T, D = x.shape
    tm = _pick_tile(seq_len, (512, 256, 128))
    tps = seq_len // tm
    assert tm % HALO == 0 and max(POOL_WINDOWS) // 2 <= HALO
    halo_blocks = tm // HALO
    n_halo = T // HALO
    block_bytes = (2 * tm * D * 4 + tm * ATTN_WIDTH * 4 + (tm + 2 * HALO) * POOL_WIDTH * 4
                   + len(POOL_WINDOWS) * POOL_GROUP_WIDTH ** 2 * 2 + (ATTN_WIDTH + POOL_WIDTH) * D * 2 + 32 * D * 4)
    return pl.pallas_call(
        functools.partial(_mix_out_kernel, tps=tps, seq_len=seq_len),
        grid=(T // tm,),
        in_specs=[
            pl.BlockSpec((tm, D), lambda i: (i, 0)),
            pl.BlockSpec((1, N_MOD, D), lambda i: (i // tps, 0, 0)),
            pl.BlockSpec((tm, ATTN_WIDTH), lambda i: (i, 0)),
            pl.BlockSpec((1, ATTN_WIDTH), lambda i: (0, 0)),
            pl.BlockSpec((tm, POOL_WIDTH), lambda i: (i, 0)),
            pl.BlockSpec((HALO, POOL_WIDTH), lambda i: (jnp.maximum(i * halo_blocks - 1, 0), 0)),
            pl.BlockSpec((HALO, POOL_WIDTH), lambda i: (jnp.minimum((i + 1) * halo_blocks, n_halo - 1), 0)),
            pl.BlockSpec((None, len(POOL_WINDOWS), POOL_GROUP_WIDTH, POOL_GROUP_WIDTH), lambda i: (layer, 0, 0, 0)),
            pl.BlockSpec((1, POOL_WIDTH), lambda i: (0, 0)),
            pl.BlockSpec((None, ATTN_WIDTH + POOL_WIDTH, D), lambda i: (layer, 0, 0)),
        ],
        out_specs=pl.BlockSpec((tm, D), lambda i: (i, 0)),
        out_shape=jax.ShapeDtypeStruct((T, D), F32),
        scratch_shapes=[
            pltpu.VMEM((tm + 2 * HALO, POOL_WIDTH), F32),
            pltpu.VMEM((tm, ATTN_WIDTH + POOL_WIDTH), BF16),
        ],
        compiler_params=pltpu.CompilerParams(
            dimension_semantics=("parallel",),
            vmem_limit_bytes=_vmem_limit(
                block_bytes, (tm + 2 * HALO) * POOL_WIDTH * 4 + tm * D * 2, 2 * tm * D * 4),
        ),
        name="mix_out",
    )(x, mod, ao, aog, u, u, u, wpool, ps, wout)


def _final_kernel(x_ref, mod_ref, g_ref, o_ref):
    o_ref[...] = _rms(x_ref[...], g_ref[...]) * (1.0 + mod_ref[0, 1:2, :]) + mod_ref[0, 0:1, :]


def _final_norm(x, mod, gain, seq_len):
    T, D = x.shape
    tm = _pick_tile(seq_len, (512, 256, 128))
    tps = seq_len // tm
    return pl.pallas_call(
        _final_kernel,
        grid=(T // tm,),
        in_specs=[
            pl.BlockSpec((tm, D), lambda i: (i, 0)),
            pl.BlockSpec((1, 2, D), lambda i: (i // tps, 0, 0)),
            pl.BlockSpec((1, D), lambda i: (0, 0)),
        ],
        out_specs=pl.BlockSpec((tm, D), lambda i: (i, 0)),
        out_shape=jax.ShapeDtypeStruct((T, D), F32),
        compiler_params=pltpu.CompilerParams(
            dimension_semantics=("parallel",),
            vmem_limit_bytes=_vmem_limit(2 * tm * D * 4 + 16 * D * 4, 0, tm * D * 4),
        ),
        name="final_norm",
    )(x, mod, gain)


def _rot_half_cols(w):
    half = w.shape[-1] // 2
    return jnp.concatenate([-w[..., half:], w[..., :half]], axis=-1)


def _rope_tables(seq_len):
    inv = 1.0 / (ROPE_BASE ** (jnp.arange(0, QK_ROPE_DIM, 2, dtype=F32) / QK_ROPE_DIM))
    ang = jnp.arange(seq_len, dtype=F32)[:, None] * inv[None, :]
    cos, sin = jnp.cos(ang), jnp.sin(ang)
    ktab = jnp.concatenate([cos, cos, sin, sin], axis=-1)
    qscale = (QK_NOPE_DIM + QK_ROPE_DIM) ** -0.5 * LOG2E
    qtab = qscale * jnp.concatenate([jnp.ones((seq_len, QK_NOPE_DIM), F32), ktab], axis=-1)
    return qtab, ktab


def _trunk(x, mod, mod_final, p):
    B, S, D = x.shape
    L = p["norm_g"].shape[0]
    qtab, ktab = _rope_tables(S)
    xf = x.reshape(B * S, D)
    for l in range(L):
        ml = mod[l]
        xf = _ffn(xf, ml, p["norm_g"][l, 0][None], p["wg"], p["wu"], p["wd"], l, 0, 0, S)
        q, k, v, u = _mix_in(xf, ml, p["norm_g"][l, 1][None], p["win"], p["q_norm_g"][l][None],
                             p["kv_norm_g"][l][None], p["wuq"], p["wukv"], qtab, ktab, l, S)
        ao = _attention(q.reshape(B, S, -1), k.reshape(B, S, -1), v.reshape(B, S, -1))
        xf = _mix_out(xf, ml, ao.reshape(B * S, -1), p["attn_out_g"][l][None], u, p["wpool"],
                      p["pool_scale"][l][None], p["wout"], l, S)
        xf = _ffn(xf, ml, p["norm_g"][l, 2][None], p["wg"], p["wu"], p["wd"], l, 1, 6, S)
    return _final_norm(xf, mod_final, p["final_g"][None], S).reshape(B, S, D)


def kernel(x_prompt, x_sample, c_prompt, c_sample, norm_g, w_mod, b_mod, ffn_w_gate, ffn_w_up, ffn_w_down, w_in, q_norm_g, kv_norm_g, w_uq, w_ukv, attn_out_g, w_pool, pool_scale, w_out, w_mod_final, b_mod_final, final_g):
    L, D = norm_g.shape[0], norm_g.shape[-1]
    bp, bs = x_prompt.shape[0], x_sample.shape[0]

    c_all = jnp.concatenate([c_prompt, c_sample], axis=0)
    mod_all = _modulation(c_all, w_mod, b_mod).reshape(L, bp + bs, N_MOD, D)
    modf_all = _modulation(c_all, w_mod_final[None], b_mod_final[None]).reshape(bp + bs, 2, D)

    c_kr = Q_LORA_RANK + KV_LORA_RANK
    c_u = c_kr + QK_ROPE_DIM
    w_kr = w_in[:, :, c_kr:c_u]
    win = jnp.concatenate([w_in[:, :, :c_u], _rot_half_cols(w_kr), w_in[:, :, c_u:]], axis=-1).astype(BF16)
    wq = w_uq.reshape(L, Q_LORA_RANK, N_HEADS, QK_NOPE_DIM + QK_ROPE_DIM)
    wq_rope = wq[..., QK_NOPE_DIM:]
    wuq = jnp.concatenate([wq, _rot_half_cols(wq_rope)], axis=-1).reshape(L, Q_LORA_RANK, N_HEADS * HEAD_QK).astype(BF16)
    p = {
        "norm_g": norm_g, "q_norm_g": q_norm_g, "kv_norm_g": kv_norm_g, "attn_out_g": attn_out_g,
        "pool_scale": pool_scale, "final_g": final_g,
        "wg": ffn_w_gate.astype(BF16), "wu": ffn_w_up.astype(BF16), "wd": ffn_w_down.astype(BF16),
        "win": win, "wuq": wuq, "wukv": w_ukv.astype(BF16),
        "wpool": w_pool.astype(BF16), "wout": w_out.astype(BF16),
    }
    y_prompt = _trunk(x_prompt, mod_all[:, :bp], modf_all[:bp], p)
    y_sample = _trunk(x_sample, mod_all[:, bp:], modf_all[bp:], p)
    return (y_prompt, y_sample)
```

```python
import functools
import math

import jax
import jax.numpy as jnp
from jax import lax
from jax.experimental import pallas as pl
from jax.experimental.pallas import tpu as pltpu

F32 = jnp.float32
BF16 = jnp.bfloat16

N_HEADS = 8
QK_NOPE_DIM = 128
QK_ROPE_DIM = 64
V_HEAD_DIM = 128
Q_LORA_RANK = 512
KV_LORA_RANK = 512
ROPE_BASE = 10000.0
POOL_WINDOWS = (2, 4, 8, 16)
POOL_GROUP_WIDTH = 256
POOL_WIDTH = POOL_GROUP_WIDTH * len(POOL_WINDOWS)
ATTN_WIDTH = N_HEADS * V_HEAD_DIM
N_MOD = 9
NORM_EPS = 1e-6
HEAD_QK = QK_NOPE_DIM + 2 * QK_ROPE_DIM
QK_WIDTH = N_HEADS * HEAD_QK
HALO = 8
LOG2E = math.log2(math.e)
NT_DIMS = (((1,), (1,)), ((), ()))

VMEM_LIMIT_CAP = 60 * 1024 * 1024


def _vmem_limit(block_bytes, scratch_bytes, temp_bytes):
    return int(min(VMEM_LIMIT_CAP, 2 * block_bytes + scratch_bytes + temp_bytes + (4 << 20)))


def _pick_tile(n, preferred):
    for t in preferred:
        if n % t == 0:
            return t
    return n


def _seq_tile(seq_len):
    return _pick_tile(seq_len, (512, 256, 128))


def _rms(x, gain):
    ms = jnp.mean(x * x, axis=-1, keepdims=True)
    return x * lax.rsqrt(ms + NORM_EPS) * gain


def _mod_kernel(c_ref, w_ref, b_ref, o_ref):
    c = c_ref[...]
    cond = (c * jax.nn.sigmoid(c)).astype(BF16)
    o_ref[...] = jnp.dot(cond, w_ref[...].astype(BF16), preferred_element_type=F32) + b_ref[...]


def _modulation(c, w, b):
    L, D, N = w.shape
    R = c.shape[0]
    tn = _pick_tile(N, (1024, 512, 256, 128))
    return pl.pallas_call(
        _mod_kernel,
        grid=(L, N // tn),
        in_specs=[
            pl.BlockSpec((R, D), lambda l, j: (0, 0)),
            pl.BlockSpec((None, D, tn), lambda l, j: (l, 0, j)),
            pl.BlockSpec((None, 1, tn), lambda l, j: (l, 0, j)),
        ],
        out_specs=pl.BlockSpec((None, R, tn), lambda l, j: (l, 0, j)),
        out_shape=jax.ShapeDtypeStruct((L, R, N), F32),
        compiler_params=pltpu.CompilerParams(
            dimension_semantics=("parallel", "parallel"),
            vmem_limit_bytes=_vmem_limit(D * tn * 4 + R * D * 4 + R * tn * 4, 0, D * tn * 2),
        ),
        name="adaln_mod",
    )(c, w, b.reshape(L, 1, N))


def _ffn_kernel(x_ref, mod_ref, g_ref, wg_ref, wu_ref, wd_ref, o_ref, h_ref, *, mod_base):
    j = pl.program_id(1)

    @pl.when(j == 0)
    def _():
        shift = mod_ref[0, mod_base:mod_base + 1, :]
        scale = mod_ref[0, mod_base + 1:mod_base + 2, :]
        h_ref[...] = (_rms(x_ref[...], g_ref[...]) * (1.0 + scale) + shift).astype(BF16)
        o_ref[...] = jnp.zeros_like(o_ref)

    h = h_ref[...]
    gate = jnp.dot(h, wg_ref[...], preferred_element_type=F32)
    up = jnp.dot(h, wu_ref[...], preferred_element_type=F32)
    act = (gate * jax.nn.sigmoid(gate) * up).astype(BF16)
    o_ref[...] += jnp.dot(act, wd_ref[...], preferred_element_type=F32)

    @pl.when(j == pl.num_programs(1) - 1)
    def _():
        res_gate = mod_ref[0, mod_base + 2:mod_base + 3, :]
        o_ref[...] = x_ref[...] + 0.5 * res_gate * o_ref[...]


def _ffn(x, mod, gain, wg, wu, wd, layer, half, mod_base, seq_len):
    T, D = x.shape
    F = wg.shape[-1]
    tm = _pick_tile(seq_len, (512, 256, 128))
    tf = _pick_tile(F, (512, 256, 128))
    tps = seq_len // tm
    return pl.pallas_call(
        functools.partial(_ffn_kernel, mod_base=mod_base),
        grid=(T // tm, F // tf),
        in_specs=[
            pl.BlockSpec((tm, D), lambda i, j: (i, 0)),
            pl.BlockSpec((1, N_MOD, D), lambda i, j: (i // tps, 0, 0)),
            pl.BlockSpec((1, D), lambda i, j: (0, 0)),
            pl.BlockSpec((None, None, D, tf), lambda i, j: (layer, half, 0, j)),
            pl.BlockSpec((None, None, D, tf), lambda i, j: (layer, half, 0, j)),
            pl.BlockSpec((None, None, tf, D), lambda i, j: (layer, half, j, 0)),
        ],
        out_specs=pl.BlockSpec((tm, D), lambda i, j: (i, 0)),
        out_shape=jax.ShapeDtypeStruct((T, D), F32),
        scratch_shapes=[pltpu.VMEM((tm, D), BF16)],
        compiler_params=pltpu.CompilerParams(
            dimension_semantics=("parallel", "arbitrary"),
            vmem_limit_bytes=_vmem_limit(
                2 * tm * D * 4 + 3 * D * tf * 2 + 16 * D * 4, tm * D * 2, 3 * tm * tf * 4 + tm * D * 4),
        ),
        name="ffn",
    )(x, mod, gain, wg, wu, wd)


def _mix_in_kernel(x_ref, mod_ref, g_ref, win_ref, qg_ref, kvg_ref, wuqt_ref, wuk_ref, wuvt_ref, qtab_ref, ktab_ref,
                   q_ref, k_ref, v_ref, u_ref):
    shift = mod_ref[0, 3:4, :]
    scale = mod_ref[0, 4:5, :]
    h = (_rms(x_ref[...], g_ref[...]) * (1.0 + scale) + shift).astype(BF16)
    z = jnp.dot(h, win_ref[...], preferred_element_type=F32)
    c_kv = Q_LORA_RANK
    c_kr = c_kv + KV_LORA_RANK
    c_u = c_kr + 2 * QK_ROPE_DIM
    u_ref[...] = z[:, c_u:]

    qn = _rms(z[:, :c_kv], qg_ref[...]).astype(BF16)
    kvn = _rms(z[:, c_kv:c_kr], kvg_ref[...]).astype(BF16)

    q_t = lax.dot_general(wuqt_ref[...], qn, NT_DIMS, preferred_element_type=F32)
    qtab = qtab_ref[...]
    for hd in range(N_HEADS):
        rows = slice(hd * HEAD_QK, (hd + 1) * HEAD_QK)
        q_ref[0, 0, rows, :] = (q_t[rows, :] * qtab).astype(BF16)
    v_t = lax.dot_general(wuvt_ref[...], kvn, NT_DIMS, preferred_element_type=F32)
    v_ref[0, 0] = v_t.astype(BF16)

    t = z[:, c_kr:c_u] * ktab_ref[...]
    k_rot = (t + pltpu.roll(t, QK_ROPE_DIM, axis=1)).astype(BF16)
    k_nope = jnp.dot(kvn, wuk_ref[...], preferred_element_type=F32)
    for hd in range(N_HEADS):
        c0 = hd * HEAD_QK
        k_ref[:, c0:c0 + QK_NOPE_DIM] = k_nope[:, hd * QK_NOPE_DIM:(hd + 1) * QK_NOPE_DIM].astype(BF16)
        k_ref[:, c0 + QK_NOPE_DIM:c0 + HEAD_QK] = k_rot


def _mix_in(x, mod, gain, win, qg, kvg, wuqt, wuk, wuvt, qtab_t, ktab, layer, batch, seq_len):
    T, D = x.shape
    n_in = win.shape[-1]
    tm = _seq_tile(seq_len)
    tps = seq_len // tm
    whole = lambda shape: pl.BlockSpec(shape, lambda i: (0,) * len(shape))
    layer_mat = lambda r, c: pl.BlockSpec((None, r, c), lambda i: (layer, 0, 0))
    block_bytes = (tm * D * 4 + D * n_in * 2 + Q_LORA_RANK * QK_WIDTH * 2 + KV_LORA_RANK * 2 * ATTN_WIDTH * 2
                   + tm * (HEAD_QK + 2 * QK_ROPE_DIM) * 4 + 2 * tm * QK_WIDTH * 2 + tm * ATTN_WIDTH * 2
                   + tm * POOL_WIDTH * 4 + 16 * D * 4)
    return pl.pallas_call(
        _mix_in_kernel,
        grid=(T // tm,),
        in_specs=[
            pl.BlockSpec((tm, D), lambda i: (i, 0)),
            pl.BlockSpec((1, N_MOD, D), lambda i: (i // tps, 0, 0)),
            whole((1, D)),
            layer_mat(D, n_in),
            whole((1, Q_LORA_RANK)),
            whole((1, KV_LORA_RANK)),
            layer_mat(QK_WIDTH, Q_LORA_RANK),
            layer_mat(KV_LORA_RANK, N_HEADS * QK_NOPE_DIM),
            layer_mat(ATTN_WIDTH, KV_LORA_RANK),
            pl.BlockSpec((HEAD_QK, tm), lambda i: (0, i % tps)),
            pl.BlockSpec((tm, 2 * QK_ROPE_DIM), lambda i: (i % tps, 0)),
        ],
        out_specs=[
            pl.BlockSpec((1, 1, QK_WIDTH, tm), lambda i: (i // tps, i % tps, 0, 0)),
            pl.BlockSpec((tm, QK_WIDTH), lambda i: (i, 0)),
            pl.BlockSpec((1, 1, ATTN_WIDTH, tm), lambda i: (i // tps, i % tps, 0, 0)),
            pl.BlockSpec((tm, POOL_WIDTH), lambda i: (i, 0)),
        ],
        out_shape=[
            jax.ShapeDtypeStruct((batch, tps, QK_WIDTH, tm), BF16),
            jax.ShapeDtypeStruct((T, QK_WIDTH), BF16),
            jax.ShapeDtypeStruct((batch, tps, ATTN_WIDTH, tm), BF16),
            jax.ShapeDtypeStruct((T, POOL_WIDTH), F32),
        ],
        compiler_params=pltpu.CompilerParams(
            dimension_semantics=("parallel",),
            vmem_limit_bytes=_vmem_limit(block_bytes, 0, tm * (n_in + 2 * QK_WIDTH) * 4 + tm * D * 4),
        ),
        name="mix_in",
    )(x, mod, gain, win, qg, kvg, wuqt, wuk, wuvt, qtab_t, ktab)


def _attn_kernel(q_ref, k_ref, v_ref, o_ref, s0_ref, s1_ref):
    n_t, tq = q_ref.shape[1], q_ref.shape[3]

    def step(t, s_cur, s_prev, m_prev):
        q_t = None if s_cur is None else q_ref[0, t]
        m = l = acc = None
        for c in range(n_t):
            rows = slice(c * tq, (c + 1) * tq)
            if s_cur is not None:
                s = jnp.dot(k_ref[0, rows, :], q_t, preferred_element_type=F32)
                s_cur[rows, :] = s
                mc = jnp.max(s, axis=0, keepdims=True)
                m = mc if m is None else jnp.maximum(m, mc)
            if s_prev is not None:
                p = jnp.exp2(s_prev[rows, :] - m_prev)
                pv = jnp.dot(v_ref[0, c], p.astype(BF16), preferred_element_type=F32)
                lc = jnp.sum(p, axis=0, keepdims=True)
                acc = pv if acc is None else acc + pv
                l = lc if l is None else l + lc
        if s_prev is not None:
            start = (t - 1) * tq
            out_rows = pl.ds(start if isinstance(start, int) else pl.multiple_of(start, tq), tq)
            o_ref[0, out_rows, :] = (acc / l).T
        return m

    m = step(0, s0_ref, None, None)

    def pair(i, m):
        m = step(2 * i + 1, s1_ref, s0_ref, m)
        return step(2 * i + 2, s0_ref, s1_ref, m)

    m = lax.fori_loop(0, (n_t - 1) // 2, pair, m)
    if n_t % 2 == 0:
        m = step(n_t - 1, s1_ref, s0_ref, m)
    step(n_t, None, s1_ref if n_t % 2 == 0 else s0_ref, m)


def _attention(q_t, k, v_t):
    B, n_t, _, tq = q_t.shape
    S = k.shape[1]
    block_bytes = S * HEAD_QK * 2 * 2 + S * V_HEAD_DIM * 2 + S * V_HEAD_DIM * 4
    scratch_bytes = 2 * S * tq * 4
    return pl.pallas_call(
        _attn_kernel,
        grid=(B, N_HEADS),
        in_specs=[
            pl.BlockSpec((1, n_t, HEAD_QK, tq), lambda b, h: (b, 0, h, 0)),
            pl.BlockSpec((1, S, HEAD_QK), lambda b, h: (b, 0, h)),
            pl.BlockSpec((1, n_t, V_HEAD_DIM, tq), lambda b, h: (b, 0, h, 0)),
        ],
        out_specs=pl.BlockSpec((1, S, V_HEAD_DIM), lambda b, h: (b, 0, h)),
        out_shape=jax.ShapeDtypeStruct((B, S, N_HEADS * V_HEAD_DIM), F32),
        scratch_shapes=[pltpu.VMEM((S, tq), F32), pltpu.VMEM((S, tq), F32)],
        compiler_params=pltpu.CompilerParams(
            dimension_semantics=("parallel", "parallel"),
            vmem_limit_bytes=_vmem_limit(block_bytes, scratch_bytes, 6 * tq * tq * 4),
        ),
        name="attn",
    )(q_t, k, v_t)


def _mix_out_kernel(x_ref, mod_ref, ao_ref, aog_ref, u_ref, up_ref, un_ref, wpool_ref, ps_ref, wout_ref,
                    o_ref, ext_ref, cat_ref, *, tps, seq_len):
    tm = x_ref.shape[0]
    si = pl.program_id(0) % tps
    cat_ref[:, :ATTN_WIDTH] = _rms(ao_ref[...], aog_ref[...]).astype(BF16)

    u = u_ref[...]
    ext_ref[0:HALO, :] = jnp.where(si == 0, 0.0, up_ref[...])
    ext_ref[HALO:HALO + tm, :] = u
    ext_ref[HALO + tm:2 * HALO + tm, :] = jnp.where(si == tps - 1, 0.0, un_ref[...])
    pos = si * tm + lax.broadcasted_iota(jnp.int32, (tm, 1), 0)
    for g, w in enumerate(POOL_WINDOWS):
        c0 = g * POOL_GROUP_WIDTH
        cols = slice(c0, c0 + POOL_GROUP_WIDTH)
        first = -(w // 2 - 1)
        win_sum = ext_ref[pl.ds(HALO + first, tm), cols]
        for d in range(first + 1, w // 2 + 1):
            win_sum = win_sum + ext_ref[pl.ds(HALO + d, tm), cols]
        lo = jnp.maximum(pos + first, 0)
        hi = jnp.minimum(pos + w // 2 + 1, seq_len)
        pooled = win_sum / (hi - lo).astype(F32) - u[:, cols]
        y = jnp.dot(pooled.astype(BF16), wpool_ref[g], preferred_element_type=F32) * ps_ref[:, cols]
        cat_ref[:, ATTN_WIDTH + c0:ATTN_WIDTH + c0 + POOL_GROUP_WIDTH] = y.astype(BF16)

    out = jnp.dot(cat_ref[...], wout_ref[...], preferred_element_type=F32)
    o_ref[...] = x_ref[...] + mod_ref[0, 5:6, :] * out


def _mix_out(x, mod, ao, aog, u, wpool, ps, wout, layer, seq_len):
    T, D = x.shape
    tm = _pick_tile(seq_len, (512, 256, 128))
    tps = seq_len // tm
    assert tm % HALO == 0 and max(POOL_WINDOWS) // 2 <= HALO
    halo_blocks = tm // HALO
    n_halo = T // HALO
    block_bytes = (2 * tm * D * 4 + tm * ATTN_WIDTH * 4 + (tm + 2 * HALO) * POOL_WIDTH * 4
                   + len(POOL_WINDOWS) * POOL_GROUP_WIDTH ** 2 * 2 + (ATTN_WIDTH + POOL_WIDTH) * D * 2 + 32 * D * 4)
    return pl.pallas_call(
        functools.partial(_mix_out_kernel, tps=tps, seq_len=seq_len),
        grid=(T // tm,),
        in_specs=[
            pl.BlockSpec((tm, D), lambda i: (i, 0)),
            pl.BlockSpec((1, N_MOD, D), lambda i: (i // tps, 0, 0)),
            pl.BlockSpec((tm, ATTN_WIDTH), lambda i: (i, 0)),
            pl.BlockSpec((1, ATTN_WIDTH), lambda i: (0, 0)),
            pl.BlockSpec((tm, POOL_WIDTH), lambda i: (i, 0)),
            pl.BlockSpec((HALO, POOL_WIDTH), lambda i: (jnp.maximum(i * halo_blocks - 1, 0), 0)),
            pl.BlockSpec((HALO, POOL_WIDTH), lambda i: (jnp.minimum((i + 1) * halo_blocks, n_halo - 1), 0)),
            pl.BlockSpec((None, len(POOL_WINDOWS), POOL_GROUP_WIDTH, POOL_GROUP_WIDTH), lambda i: (layer, 0, 0, 0)),
            pl.BlockSpec((1, POOL_WIDTH), lambda i: (0, 0)),
            pl.BlockSpec((None, ATTN_WIDTH + POOL_WIDTH, D), lambda i: (layer, 0, 0)),
        ],
        out_specs=pl.BlockSpec((tm, D), lambda i: (i, 0)),
        out_shape=jax.ShapeDtypeStruct((T, D), F32),
        scratch_shapes=[
            pltpu.VMEM((tm + 2 * HALO, POOL_WIDTH), F32),
            pltpu.VMEM((tm, ATTN_WIDTH + POOL_WIDTH), BF16),
        ],
        compiler_params=pltpu.CompilerParams(
            dimension_semantics=("parallel",),
            vmem_limit_bytes=_vmem_limit(
                block_bytes, (tm + 2 * HALO) * POOL_WIDTH * 4 + tm * D * 2, 2 * tm * D * 4),
        ),
        name="mix_out",
    )(x, mod, ao, aog, u, u, u, wpool, ps, wout)


def _final_kernel(x_ref, mod_ref, g_ref, o_ref):
    o_ref[...] = _rms(x_ref[...], g_ref[...]) * (1.0 + mod_ref[0, 1:2, :]) + mod_ref[0, 0:1, :]


def _final_norm(x, mod, gain, seq_len):
    T, D = x.shape
    tm = _pick_tile(seq_len, (512, 256, 128))
    tps = seq_len // tm
    return pl.pallas_call(
        _final_kernel,
        grid=(T // tm,),
        in_specs=[
            pl.BlockSpec((tm, D), lambda i: (i, 0)),
            pl.BlockSpec((1, 2, D), lambda i: (i // tps, 0, 0)),
            pl.BlockSpec((1, D), lambda i: (0, 0)),
        ],
        out_specs=pl.BlockSpec((tm, D), lambda i: (i, 0)),
        out_shape=jax.ShapeDtypeStruct((T, D), F32),
        compiler_params=pltpu.CompilerParams(
            dimension_semantics=("parallel",),
            vmem_limit_bytes=_vmem_limit(2 * tm * D * 4 + 16 * D * 4, 0, tm * D * 4),
        ),
        name="final_norm",
    )(x, mod, gain)


def _rot_half_cols(w):
    half = w.shape[-1] // 2
    return jnp.concatenate([-w[..., half:], w[..., :half]], axis=-1)


def _rope_tables(seq_len):
    inv = 1.0 / (ROPE_BASE ** (jnp.arange(0, QK_ROPE_DIM, 2, dtype=F32) / QK_ROPE_DIM))
    ang = jnp.arange(seq_len, dtype=F32)[:, None] * inv[None, :]
    cos, sin = jnp.cos(ang), jnp.sin(ang)
    ktab = jnp.concatenate([cos, cos, sin, sin], axis=-1)
    qscale = (QK_NOPE_DIM + QK_ROPE_DIM) ** -0.5 * LOG2E
    qtab = qscale * jnp.concatenate([jnp.ones((seq_len, QK_NOPE_DIM), F32), ktab], axis=-1)
    return qtab.T, ktab


def _trunk(x, mod, mod_final, p):
    B, S, D = x.shape
    L = p["norm_g"].shape[0]
    qtab_t, ktab = _rope_tables(S)
    xf = x.reshape(B * S, D)
    for l in range(L):
        ml = mod[l]
        xf = _ffn(xf, ml, p["norm_g"][l, 0][None], p["wg"], p["wu"], p["wd"], l, 0, 0, S)
        q_t, k, v_t, u = _mix_in(xf, ml, p["norm_g"][l, 1][None], p["win"], p["q_norm_g"][l][None],
                                 p["kv_norm_g"][l][None], p["wuqt"], p["wuk"], p["wuvt"], qtab_t, ktab, l, B, S)
        ao = _attention(q_t, k.reshape(B, S, QK_WIDTH), v_t)
        xf = _mix_out(xf, ml, ao.reshape(B * S, ATTN_WIDTH), p["attn_out_g"][l][None], u, p["wpool"],
                      p["pool_scale"][l][None], p["wout"], l, S)
        xf = _ffn(xf, ml, p["norm_g"][l, 2][None], p["wg"], p["wu"], p["wd"], l, 1, 6, S)
    return _final_norm(xf, mod_final, p["final_g"][None], S).reshape(B, S, D)


def kernel(x_prompt, x_sample, c_prompt, c_sample, norm_g, w_mod, b_mod, ffn_w_gate, ffn_w_up, ffn_w_down, w_in, q_norm_g, kv_norm_g, w_uq, w_ukv, attn_out_g, w_pool, pool_scale, w_out, w_mod_final, b_mod_final, final_g):
    L, D = norm_g.shape[0], norm_g.shape[-1]
    bp, bs = x_prompt.shape[0], x_sample.shape[0]

    c_all = jnp.concatenate([c_prompt, c_sample], axis=0)
    mod_all = _modulation(c_all, w_mod, b_mod).reshape(L, bp + bs, N_MOD, D)
    modf_all = _modulation(c_all, w_mod_final[None], b_mod_final[None]).reshape(bp + bs, 2, D)

    c_kr = Q_LORA_RANK + KV_LORA_RANK
    c_u = c_kr + QK_ROPE_DIM
    w_kr = w_in[:, :, c_kr:c_u]
    win = jnp.concatenate([w_in[:, :, :c_u], _rot_half_cols(w_kr), w_in[:, :, c_u:]], axis=-1).astype(BF16)
    wq = w_uq.reshape(L, Q_LORA_RANK, N_HEADS, QK_NOPE_DIM + QK_ROPE_DIM)
    wq = jnp.concatenate([wq, _rot_half_cols(wq[..., QK_NOPE_DIM:])], axis=-1)
    wuqt = jnp.swapaxes(wq.reshape(L, Q_LORA_RANK, QK_WIDTH), 1, 2).astype(BF16)
    wkv = w_ukv.reshape(L, KV_LORA_RANK, N_HEADS, QK_NOPE_DIM + V_HEAD_DIM)
    wuk = wkv[..., :QK_NOPE_DIM].reshape(L, KV_LORA_RANK, N_HEADS * QK_NOPE_DIM).astype(BF16)
    wuvt = jnp.swapaxes(wkv[..., QK_NOPE_DIM:].reshape(L, KV_LORA_RANK, ATTN_WIDTH), 1, 2).astype(BF16)
    p = {
        "norm_g": norm_g, "q_norm_g": q_norm_g, "kv_norm_g": kv_norm_g, "attn_out_g": attn_out_g,
        "pool_scale": pool_scale, "final_g": final_g,
        "wg": ffn_w_gate.astype(BF16), "wu": ffn_w_up.astype(BF16), "wd": ffn_w_down.astype(BF16),
        "win": win, "wuqt": wuqt, "wuk": wuk, "wuvt": wuvt,
        "wpool": w_pool.astype(BF16), "wout": w_out.astype(BF16),
    }
    y_prompt = _trunk(x_prompt, mod_all[:, :bp], modf_all[:bp], p)
    y_sample = _trunk(x_sample, mod_all[:, bp:], modf_all[bp:], p)
    return (y_prompt, y_sample)
```

```python
import functools
import math

import jax
import jax.numpy as jnp
from jax import lax
from jax.experimental import pallas as pl
from jax.experimental.pallas import tpu as pltpu

F32 = jnp.float32
BF16 = jnp.bfloat16

N_HEADS = 8
QK_NOPE_DIM = 128
QK_ROPE_DIM = 64
V_HEAD_DIM = 128
Q_LORA_RANK = 512
KV_LORA_RANK = 512
ROPE_BASE = 10000.0
POOL_WINDOWS = (2, 4, 8, 16)
POOL_GROUP_WIDTH = 256
POOL_WIDTH = POOL_GROUP_WIDTH * len(POOL_WINDOWS)
ATTN_WIDTH = N_HEADS * V_HEAD_DIM
N_MOD = 9
NORM_EPS = 1e-6
HEAD_QK = QK_NOPE_DIM + 2 * QK_ROPE_DIM
QK_WIDTH = N_HEADS * HEAD_QK
LANES = 128
NORM_ROWS = 64
HALO = 16
BAND_ROWS = 256
LOG2E = math.log2(math.e)
NT_DIMS = (((1,), (1,)), ((), ()))

VMEM_LIMIT_CAP = 60 * 1024 * 1024


def _vmem_limit(block_bytes, scratch_bytes, temp_bytes):
    return int(min(VMEM_LIMIT_CAP, 2 * block_bytes + scratch_bytes + temp_bytes + (4 << 20)))


def _pick_tile(n, preferred):
    for t in preferred:
        if n % t == 0:
            return t
    return n


def _seq_tile(seq_len):
    return _pick_tile(seq_len, (512, 256, 128))


def _rms(x, gain):
    ms = jnp.mean(x * x, axis=-1, keepdims=True)
    return x * lax.rsqrt(ms + NORM_EPS) * gain


def _mod_kernel(c_ref, w_ref, b_ref, o_ref):
    c = c_ref[...]
    cond = (c * jax.nn.sigmoid(c)).astype(BF16)
    o_ref[...] = jnp.dot(cond, w_ref[...].astype(BF16), preferred_element_type=F32) + b_ref[...]


def _modulation(c, w, b):
    L, D, N = w.shape
    R = c.shape[0]
    tn = _pick_tile(N, (1024, 512, 256, 128))
    return pl.pallas_call(
        _mod_kernel,
        grid=(L, N // tn),
        in_specs=[
            pl.BlockSpec((R, D), lambda l, j: (0, 0)),
            pl.BlockSpec((None, D, tn), lambda l, j: (l, 0, j)),
            pl.BlockSpec((None, 1, tn), lambda l, j: (l, 0, j)),
        ],
        out_specs=pl.BlockSpec((None, R, tn), lambda l, j: (l, 0, j)),
        out_shape=jax.ShapeDtypeStruct((L, R, N), F32),
        compiler_params=pltpu.CompilerParams(
            dimension_semantics=("parallel", "parallel"),
            vmem_limit_bytes=_vmem_limit(D * tn * 4 + R * D * 4 + R * tn * 4, 0, D * tn * 2),
        ),
        name="adaln_mod",
    )(c, w, b.reshape(L, 1, N))


def _ffn_kernel(x_ref, mod_ref, g_ref, wg_ref, wu_ref, wd_ref, o_ref, h_ref, inv_ref, *, mod_base):
    j = pl.program_id(1)

    @pl.when(j == 0)
    def _():
        tm, d = x_ref.shape
        lane_chunks = [slice(c, c + LANES) for c in range(0, d, LANES)]
        shift = mod_ref[0, mod_base:mod_base + 1, :]
        gain = g_ref[...] * (1.0 + mod_ref[0, mod_base + 1:mod_base + 2, :])

        def row_stats(r, carry):
            rows = pl.ds(pl.multiple_of(r * NORM_ROWS, NORM_ROWS), NORM_ROWS)
            ssq = None
            for cols in lane_chunks:
                xc = x_ref[rows, cols]
                ssq = xc * xc if ssq is None else ssq + xc * xc
            ms = jnp.sum(ssq, axis=-1, keepdims=True) * (1.0 / d)
            inv_ref[rows, :] = jnp.broadcast_to(lax.rsqrt(ms + NORM_EPS), (NORM_ROWS, LANES))
            return carry

        lax.fori_loop(0, tm // NORM_ROWS, row_stats, 0, unroll=2)

        for cols in lane_chunks:
            xc = x_ref[:, cols]
            h_ref[:, cols] = (xc * inv_ref[...] * gain[:, cols] + shift[:, cols]).astype(BF16)
            o_ref[:, cols] = xc

    h = h_ref[...]
    gate = jnp.dot(h, wg_ref[...], preferred_element_type=F32)
    up = jnp.dot(h, wu_ref[...], preferred_element_type=F32)
    act = (gate * jax.nn.sigmoid(gate) * up).astype(BF16)
    res_gate = 0.5 * mod_ref[0, mod_base + 2:mod_base + 3, :]
    o_ref[...] += res_gate * jnp.dot(act, wd_ref[...], preferred_element_type=F32)


def _ffn(x, mod, gain, wg, wu, wd, layer, half, mod_base, seq_len):
    T, D = x.shape
    F = wg.shape[-1]
    tm = _pick_tile(seq_len, (1024, 512, 256, 128))
    tf = _pick_tile(F, (512, 256, 128))
    tps = seq_len // tm
    return pl.pallas_call(
        functools.partial(_ffn_kernel, mod_base=mod_base),
        grid=(T // tm, F // tf),
        in_specs=[
            pl.BlockSpec((tm, D), lambda i, j: (i, 0)),
            pl.BlockSpec((1, N_MOD, D), lambda i, j: (i // tps, 0, 0)),
            pl.BlockSpec((1, D), lambda i, j: (0, 0)),
            pl.BlockSpec((None, None, D, tf), lambda i, j: (layer, half, 0, j)),
            pl.BlockSpec((None, None, D, tf), lambda i, j: (layer, half, 0, j)),
            pl.BlockSpec((None, None, tf, D), lambda i, j: (layer, half, j, 0)),
        ],
        out_specs=pl.BlockSpec((tm, D), lambda i, j: (i, 0)),
        out_shape=jax.ShapeDtypeStruct((T, D), F32),
        scratch_shapes=[pltpu.VMEM((tm, D), BF16), pltpu.VMEM((tm, LANES), F32)],
        compiler_params=pltpu.CompilerParams(
            dimension_semantics=("parallel", "arbitrary"),
            vmem_limit_bytes=_vmem_limit(
                2 * tm * D * 4 + 3 * D * tf * 2 + 16 * D * 4, tm * D * 2 + tm * LANES * 4, 3 * tm * tf * 4),
        ),
        name="ffn",
    )(x, mod, gain, wg, wu, wd)


def _mix_in_kernel(x_ref, mod_ref, g_ref, win_ref, qg_ref, kvg_ref, wuqt_ref, wuk_ref, wuvt_ref, qtab_ref, ktab_ref,
                   q_ref, k_ref, v_ref, u_ref):
    shift = mod_ref[0, 3:4, :]
    scale = mod_ref[0, 4:5, :]
    h = (_rms(x_ref[...], g_ref[...]) * (1.0 + scale) + shift).astype(BF16)
    z = jnp.dot(h, win_ref[...], preferred_element_type=F32)
    c_kv = Q_LORA_RANK
    c_kr = c_kv + KV_LORA_RANK
    c_u = c_kr + 2 * QK_ROPE_DIM
    u_ref[...] = z[:, c_u:]

    qn = _rms(z[:, :c_kv], qg_ref[...]).astype(BF16)
    kvn = _rms(z[:, c_kv:c_kr], kvg_ref[...]).astype(BF16)

    q_t = lax.dot_general(wuqt_ref[...], qn, NT_DIMS, preferred_element_type=F32)
    qtab = qtab_ref[...]
    for hd in range(N_HEADS):
        rows = slice(hd * HEAD_QK, (hd + 1) * HEAD_QK)
        q_ref[0, 0, rows, :] = (q_t[rows, :] * qtab).astype(BF16)
    v_t = lax.dot_general(wuvt_ref[...], kvn, NT_DIMS, preferred_element_type=F32)
    v_ref[0, 0] = v_t.astype(BF16)

    t = z[:, c_kr:c_u] * ktab_ref[...]
    k_rot = (t + pltpu.roll(t, QK_ROPE_DIM, axis=1)).astype(BF16)
    k_nope = jnp.dot(kvn, wuk_ref[...], preferred_element_type=F32)
    for hd in range(N_HEADS):
        c0 = hd * HEAD_QK
        k_ref[:, c0:c0 + QK_NOPE_DIM] = k_nope[:, hd * QK_NOPE_DIM:(hd + 1) * QK_NOPE_DIM].astype(BF16)
        k_ref[:, c0 + QK_NOPE_DIM:c0 + HEAD_QK] = k_rot


def _mix_in(x, mod, gain, win, qg, kvg, wuqt, wuk, wuvt, qtab_t, ktab, layer, batch, seq_len):
    T, D = x.shape
    n_in = win.shape[-1]
    tm = _seq_tile(seq_len)
    tps = seq_len // tm
    whole = lambda shape: pl.BlockSpec(shape, lambda i: (0,) * len(shape))
    layer_mat = lambda r, c: pl.BlockSpec((None, r, c), lambda i: (layer, 0, 0))
    block_bytes = (tm * D * 4 + D * n_in * 2 + Q_LORA_RANK * QK_WIDTH * 2 + KV_LORA_RANK * 2 * ATTN_WIDTH * 2
                   + tm * (HEAD_QK + 2 * QK_ROPE_DIM) * 4 + 2 * tm * QK_WIDTH * 2 + tm * ATTN_WIDTH * 2
                   + tm * POOL_WIDTH * 4 + 16 * D * 4)
    return pl.pallas_call(
        _mix_in_kernel,
        grid=(T // tm,),
        in_specs=[
            pl.BlockSpec((tm, D), lambda i: (i, 0)),
            pl.BlockSpec((1, N_MOD, D), lambda i: (i // tps, 0, 0)),
            whole((1, D)),
            layer_mat(D, n_in),
            whole((1, Q_LORA_RANK)),
            whole((1, KV_LORA_RANK)),
            layer_mat(QK_WIDTH, Q_LORA_RANK),
            layer_mat(KV_LORA_RANK, N_HEADS * QK_NOPE_DIM),
            layer_mat(ATTN_WIDTH, KV_LORA_RANK),
            pl.BlockSpec((HEAD_QK, tm), lambda i: (0, i % tps)),
            pl.BlockSpec((tm, 2 * QK_ROPE_DIM), lambda i: (i % tps, 0)),
        ],
        out_specs=[
            pl.BlockSpec((1, 1, QK_WIDTH, tm), lambda i: (i // tps, i % tps, 0, 0)),
            pl.BlockSpec((tm, QK_WIDTH), lambda i: (i, 0)),
            pl.BlockSpec((1, 1, ATTN_WIDTH, tm), lambda i: (i // tps, i % tps, 0, 0)),
            pl.BlockSpec((tm, POOL_WIDTH), lambda i: (i, 0)),
        ],
        out_shape=[
            jax.ShapeDtypeStruct((batch, tps, QK_WIDTH, tm), BF16),
            jax.ShapeDtypeStruct((T, QK_WIDTH), BF16),
            jax.ShapeDtypeStruct((batch, tps, ATTN_WIDTH, tm), BF16),
            jax.ShapeDtypeStruct((T, POOL_WIDTH), F32),
        ],
        compiler_params=pltpu.CompilerParams(
            dimension_semantics=("parallel",),
            vmem_limit_bytes=_vmem_limit(block_bytes, 0, tm * (n_in + 2 * QK_WIDTH) * 4 + tm * D * 4),
        ),
        name="mix_in",
    )(x, mod, gain, win, qg, kvg, wuqt, wuk, wuvt, qtab_t, ktab)


def _attn_kernel(q_ref, k_ref, v_ref, o_ref, s0_ref, s1_ref):
    n_t, tq = q_ref.shape[1], q_ref.shape[3]

    def step(t, s_cur, s_prev, m_prev):
        q_t = None if s_cur is None else q_ref[0, t]
        m = l = acc = None
        for c in range(n_t):
            rows = slice(c * tq, (c + 1) * tq)
            if s_cur is not None:
                s = jnp.dot(k_ref[0, rows, :], q_t, preferred_element_type=F32)
                s_cur[rows, :] = s
                mc = jnp.max(s, axis=0, keepdims=True)
                m = mc if m is None else jnp.maximum(m, mc)
            if s_prev is not None:
                p = jnp.exp2(s_prev[rows, :] - m_prev)
                pv = jnp.dot(v_ref[0, c], p.astype(BF16), preferred_element_type=F32)
                lc = jnp.sum(p, axis=0, keepdims=True)
                acc = pv if acc is None else acc + pv
                l = lc if l is None else l + lc
        if s_prev is not None:
            start = (t - 1) * tq
            out_rows = pl.ds(start if isinstance(start, int) else pl.multiple_of(start, tq), tq)
            o_ref[0, out_rows, :] = (acc / l).T
        return m

    m = step(0, s0_ref, None, None)

    def pair(i, m):
        m = step(2 * i + 1, s1_ref, s0_ref, m)
        return step(2 * i + 2, s0_ref, s1_ref, m)

    m = lax.fori_loop(0, (n_t - 1) // 2, pair, m)
    if n_t % 2 == 0:
        m = step(n_t - 1, s1_ref, s0_ref, m)
    step(n_t, None, s1_ref if n_t % 2 == 0 else s0_ref, m)


def _attention(q_t, k, v_t):
    B, n_t, _, tq = q_t.shape
    S = k.shape[1]
    block_bytes = S * HEAD_QK * 2 * 2 + S * V_HEAD_DIM * 2 + S * V_HEAD_DIM * 4
    scratch_bytes = 2 * S * tq * 4
    return pl.pallas_call(
        _attn_kernel,
        grid=(B, N_HEADS),
        in_specs=[
            pl.BlockSpec((1, n_t, HEAD_QK, tq), lambda b, h: (b, 0, h, 0)),
            pl.BlockSpec((1, S, HEAD_QK), lambda b, h: (b, 0, h)),
            pl.BlockSpec((1, n_t, V_HEAD_DIM, tq), lambda b, h: (b, 0, h, 0)),
        ],
        out_specs=pl.BlockSpec((1, S, V_HEAD_DIM), lambda b, h: (b, 0, h)),
        out_shape=jax.ShapeDtypeStruct((B, S, N_HEADS * V_HEAD_DIM), F32),
        scratch_shapes=[pltpu.VMEM((S, tq), F32), pltpu.VMEM((S, tq), F32)],
        compiler_params=pltpu.CompilerParams(
            dimension_semantics=("parallel", "parallel"),
            vmem_limit_bytes=_vmem_limit(block_bytes, scratch_bytes, 6 * tq * tq * 4),
        ),
        name="attn",
    )(q_t, k, v_t)


def _mix_out_kernel(x_ref, mod_ref, ao_ref, aog_ref, u_ref, up_ref, un_ref, band_ref, wpool_ref, ps_ref, wout_ref,
                    o_ref, ext_ref, cat_ref, *, tps, seq_len):
    tm = x_ref.shape[0]
    si = pl.program_id(0) % tps
    cat_ref[:, :ATTN_WIDTH] = _rms(ao_ref[...], aog_ref[...]).astype(BF16)

    u = u_ref[...]
    ext_ref[0:HALO, :] = jnp.where(si == 0, 0.0, up_ref[...]).astype(BF16)
    ext_ref[HALO:HALO + tm, :] = u.astype(BF16)
    ext_ref[HALO + tm:2 * HALO + tm, :] = jnp.where(si == tps - 1, 0.0, un_ref[...]).astype(BF16)
    ext_ref[2 * HALO + tm:, :] = jnp.zeros((BAND_ROWS - 2 * HALO, POOL_WIDTH), BF16)
    pos = si * tm + lax.broadcasted_iota(jnp.int32, (tm, 1), 0)
    for g, w in enumerate(POOL_WINDOWS):
        c0 = g * POOL_GROUP_WIDTH
        cols = slice(c0, c0 + POOL_GROUP_WIDTH)
        win_sum = jnp.concatenate(
            [jnp.dot(band_ref[g], ext_ref[r:r + 2 * BAND_ROWS, cols], preferred_element_type=F32)
             for r in range(0, tm, BAND_ROWS)], axis=0)
        lo = jnp.maximum(pos - (w // 2 - 1), 0)
        hi = jnp.minimum(pos + w // 2 + 1, seq_len)
        pooled = win_sum / (hi - lo).astype(F32) - u[:, cols]
        y = jnp.dot(pooled.astype(BF16), wpool_ref[g], preferred_element_type=F32) * ps_ref[:, cols]
        cat_ref[:, ATTN_WIDTH + c0:ATTN_WIDTH + c0 + POOL_GROUP_WIDTH] = y.astype(BF16)

    out = jnp.dot(cat_ref[...], wout_ref[...], preferred_element_type=F32)
    o_ref[...] = x_ref[...] + mod_ref[0, 5:6, :] * out


def _pool_bands():
    t = jnp.arange(BAND_ROWS)[:, None]
    s = jnp.arange(2 * BAND_ROWS)[None, :] - HALO
    return jnp.stack([(s >= t - (w // 2 - 1)) & (s <= t + w // 2) for w in POOL_WINDOWS]).astype(BF16)


def _mix_out(x, mod, ao, aog, u, wpool, ps, wout, layer, seq_len):
    T, D = x.shape
    tm = _pick_tile(seq_len, (512, 256))
    tps = seq_len // tm
    assert tm % BAND_ROWS == 0 and max(POOL_WINDOWS) // 2 <= HALO
    halo_blocks = tm // HALO
    n_halo = T // HALO
    n_win = len(POOL_WINDOWS)
    ext_rows = tm + BAND_ROWS
    block_bytes = (2 * tm * D * 4 + tm * ATTN_WIDTH * 4 + (tm + 2 * HALO) * POOL_WIDTH * 4
                   + n_win * BAND_ROWS * 2 * BAND_ROWS * 2 + n_win * POOL_GROUP_WIDTH ** 2 * 2
                   + (ATTN_WIDTH + POOL_WIDTH) * D * 2 + 32 * D * 4)
    return pl.pallas_call(
        functools.partial(_mix_out_kernel, tps=tps, seq_len=seq_len),
        grid=(T // tm,),
        in_specs=[
            pl.BlockSpec((tm, D), lambda i: (i, 0)),
            pl.BlockSpec((1, N_MOD, D), lambda i: (i // tps, 0, 0)),
            pl.BlockSpec((tm, ATTN_WIDTH), lambda i: (i, 0)),
            pl.BlockSpec((1, ATTN_WIDTH), lambda i: (0, 0)),
            pl.BlockSpec((tm, POOL_WIDTH), lambda i: (i, 0)),
            pl.BlockSpec((HALO, POOL_WIDTH), lambda i: (jnp.maximum(i * halo_blocks - 1, 0), 0)),
            pl.BlockSpec((HALO, POOL_WIDTH), lambda i: (jnp.minimum((i + 1) * halo_blocks, n_halo - 1), 0)),
            pl.BlockSpec((n_win, BAND_ROWS, 2 * BAND_ROWS), lambda i: (0, 0, 0)),
            pl.BlockSpec((None, n_win, POOL_GROUP_WIDTH, POOL_GROUP_WIDTH), lambda i: (layer, 0, 0, 0)),
            pl.BlockSpec((1, POOL_WIDTH), lambda i: (0, 0)),
            pl.BlockSpec((None, ATTN_WIDTH + POOL_WIDTH, D), lambda i: (layer, 0, 0)),
        ],
        out_specs=pl.BlockSpec((tm, D), lambda i: (i, 0)),
        out_shape=jax.ShapeDtypeStruct((T, D), F32),
        scratch_shapes=[
            pltpu.VMEM((ext_rows, POOL_WIDTH), BF16),
            pltpu.VMEM((tm, ATTN_WIDTH + POOL_WIDTH), BF16),
        ],
        compiler_params=pltpu.CompilerParams(
            dimension_semantics=("parallel",),
            vmem_limit_bytes=_vmem_limit(block_bytes, ext_rows * POOL_WIDTH * 2 + tm * D * 2, 2 * tm * D * 4),
        ),
        name="mix_out",
    )(x, mod, ao, aog, u, u, u, _pool_bands(), wpool, ps, wout)


def _final_kernel(x_ref, mod_ref, g_ref, o_ref):
    o_ref[...] = _rms(x_ref[...], g_ref[...]) * (1.0 + mod_ref[0, 1:2, :]) + mod_ref[0, 0:1, :]


def _final_norm(x, mod, gain, seq_len):
    T, D = x.shape
    tm = _pick_tile(seq_len, (512, 256, 128))
    tps = seq_len // tm
    return pl.pallas_call(
        _final_kernel,
        grid=(T // tm,),
        in_specs=[
            pl.BlockSpec((tm, D), lambda i: (i, 0)),
            pl.BlockSpec((1, 2, D), lambda i: (i // tps, 0, 0)),
            pl.BlockSpec((1, D), lambda i: (0, 0)),
        ],
        out_specs=pl.BlockSpec((tm, D), lambda i: (i, 0)),
        out_shape=jax.ShapeDtypeStruct((T, D), F32),
        compiler_params=pltpu.CompilerParams(
            dimension_semantics=("parallel",),
            vmem_limit_bytes=_vmem_limit(2 * tm * D * 4 + 16 * D * 4, 0, tm * D * 4),
        ),
        name="final_norm",
    )(x, mod, gain)


def _rot_half_cols(w):
    half = w.shape[-1] // 2
    return jnp.concatenate([-w[..., half:], w[..., :half]], axis=-1)


def _rope_tables(seq_len):
    inv = 1.0 / (ROPE_BASE ** (jnp.arange(0, QK_ROPE_DIM, 2, dtype=F32) / QK_ROPE_DIM))
    ang = jnp.arange(seq_len, dtype=F32)[:, None] * inv[None, :]
    cos, sin = jnp.cos(ang), jnp.sin(ang)
    ktab = jnp.concatenate([cos, cos, sin, sin], axis=-1)
    qscale = (QK_NOPE_DIM + QK_ROPE_DIM) ** -0.5 * LOG2E
    qtab = qscale * jnp.concatenate([jnp.ones((seq_len, QK_NOPE_DIM), F32), ktab], axis=-1)
    return qtab.T, ktab


def _trunk(x, mod, mod_final, p):
    B, S, D = x.shape
    L = p["norm_g"].shape[0]
    qtab_t, ktab = _rope_tables(S)
    xf = x.reshape(B * S, D)
    for l in range(L):
        ml = mod[l]
        xf = _ffn(xf, ml, p["norm_g"][l, 0][None], p["wg"], p["wu"], p["wd"], l, 0, 0, S)
        q_t, k, v_t, u = _mix_in(xf, ml, p["norm_g"][l, 1][None], p["win"], p["q_norm_g"][l][None],
                                 p["kv_norm_g"][l][None], p["wuqt"], p["wuk"], p["wuvt"], qtab_t, ktab, l, B, S)
        ao = _attention(q_t, k.reshape(B, S, QK_WIDTH), v_t)
        xf = _mix_out(xf, ml, ao.reshape(B * S, ATTN_WIDTH), p["attn_out_g"][l][None], u, p["wpool"],
                      p["pool_scale"][l][None], p["wout"], l, S)
        xf = _ffn(xf, ml, p["norm_g"][l, 2][None], p["wg"], p["wu"], p["wd"], l, 1, 6, S)
    return _final_norm(xf, mod_final, p["final_g"][None], S).reshape(B, S, D)


def kernel(x_prompt, x_sample, c_prompt, c_sample, norm_g, w_mod, b_mod, ffn_w_gate, ffn_w_up, ffn_w_down, w_in, q_norm_g, kv_norm_g, w_uq, w_ukv, attn_out_g, w_pool, pool_scale, w_out, w_mod_final, b_mod_final, final_g):
    L, D = norm_g.shape[0], norm_g.shape[-1]
    bp, bs = x_prompt.shape[0], x_sample.shape[0]

    c_all = jnp.concatenate([c_prompt, c_sample], axis=0)
    mod_all = _modulation(c_all, w_mod, b_mod).reshape(L, bp + bs, N_MOD, D)
    modf_all = _modulation(c_all, w_mod_final[None], b_mod_final[None]).reshape(bp + bs, 2, D)

    c_kr = Q_LORA_RANK + KV_LORA_RANK
    c_u = c_kr + QK_ROPE_DIM
    w_kr = w_in[:, :, c_kr:c_u]
    win = jnp.concatenate([w_in[:, :, :c_u], _rot_half_cols(w_kr), w_in[:, :, c_u:]], axis=-1).astype(BF16)
    wq = w_uq.reshape(L, Q_LORA_RANK, N_HEADS, QK_NOPE_DIM + QK_ROPE_DIM)
    wq = jnp.concatenate([wq, _rot_half_cols(wq[..., QK_NOPE_DIM:])], axis=-1)
    wuqt = jnp.swapaxes(wq.reshape(L, Q_LORA_RANK, QK_WIDTH), 1, 2).astype(BF16)
    wkv = w_ukv.reshape(L, KV_LORA_RANK, N_HEADS, QK_NOPE_DIM + V_HEAD_DIM)
    wuk = wkv[..., :QK_NOPE_DIM].reshape(L, KV_LORA_RANK, N_HEADS * QK_NOPE_DIM).astype(BF16)
    wuvt = jnp.swapaxes(wkv[..., QK_NOPE_DIM:].reshape(L, KV_LORA_RANK, ATTN_WIDTH), 1, 2).astype(BF16)
    p = {
        "norm_g": norm_g, "q_norm_g": q_norm_g, "kv_norm_g": kv_norm_g, "attn_out_g": attn_out_g,
        "pool_scale": pool_scale, "final_g": final_g,
        "wg": ffn_w_gate.astype(BF16), "wu": ffn_w_up.astype(BF16), "wd": ffn_w_down.astype(BF16),
        "win": win, "wuqt": wuqt, "wuk": wuk, "wuvt": wuvt,
        "wpool": w_pool.astype(BF16), "wout": w_out.astype(BF16),
    }
    y_prompt = _trunk(x_prompt, mod_all[:, :bp], modf_all[:bp], p)
    y_sample = _trunk(x_sample, mod_all[:, bp:], modf_all[bp:], p)
    return (y_prompt, y_sample)
```

```python
import functools
import math

import jax
import jax.numpy as jnp
from jax import lax
from jax.experimental import pallas as pl
from jax.experimental.pallas import tpu as pltpu

F32 = jnp.float32
BF16 = jnp.bfloat16

N_HEADS = 8
QK_NOPE_DIM = 128
QK_ROPE_DIM = 64
V_HEAD_DIM = 128
Q_LORA_RANK = 512
KV_LORA_RANK = 512
ROPE_BASE = 10000.0
POOL_WINDOWS = (2, 4, 8, 16)
POOL_GROUP_WIDTH = 256
POOL_WIDTH = POOL_GROUP_WIDTH * len(POOL_WINDOWS)
ATTN_WIDTH = N_HEADS * V_HEAD_DIM
N_MOD = 9
NORM_EPS = 1e-6
HEAD_QK = QK_NOPE_DIM + 2 * QK_ROPE_DIM
QK_WIDTH = N_HEADS * HEAD_QK
LANES = 128
FIRST_STEP_ROWS = 256
HALO = 16
BAND_ROWS = 256
LOG2E = math.log2(math.e)
NT_DIMS = (((1,), (1,)), ((), ()))

VMEM_LIMIT_CAP = 60 * 1024 * 1024


def _vmem_limit(block_bytes, scratch_bytes, temp_bytes):
    return int(min(VMEM_LIMIT_CAP, 2 * block_bytes + scratch_bytes + temp_bytes + (4 << 20)))


def _pick_tile(n, preferred):
    for t in preferred:
        if n % t == 0:
            return t
    return n


def _seq_tile(seq_len):
    return _pick_tile(seq_len, (512, 256, 128))


def _rms(x, gain):
    ms = jnp.mean(x * x, axis=-1, keepdims=True)
    return x * lax.rsqrt(ms + NORM_EPS) * gain


def _mod_kernel(c_ref, w_ref, b_ref, o_ref):
    c = c_ref[...]
    cond = (c * jax.nn.sigmoid(c)).astype(BF16)
    o_ref[...] = jnp.dot(cond, w_ref[...].astype(BF16), preferred_element_type=F32) + b_ref[...]


def _modulation(c, w, b):
    L, D, N = w.shape
    R = c.shape[0]
    tn = _pick_tile(N, (1024, 512, 256, 128))
    return pl.pallas_call(
        _mod_kernel,
        grid=(L, N // tn),
        in_specs=[
            pl.BlockSpec((R, D), lambda l, j: (0, 0)),
            pl.BlockSpec((None, D, tn), lambda l, j: (l, 0, j)),
            pl.BlockSpec((None, 1, tn), lambda l, j: (l, 0, j)),
        ],
        out_specs=pl.BlockSpec((None, R, tn), lambda l, j: (l, 0, j)),
        out_shape=jax.ShapeDtypeStruct((L, R, N), F32),
        compiler_params=pltpu.CompilerParams(
            dimension_semantics=("parallel", "parallel"),
            vmem_limit_bytes=_vmem_limit(D * tn * 4 + R * D * 4 + R * tn * 4, 0, D * tn * 2),
        ),
        name="adaln_mod",
    )(c, w, b.reshape(L, 1, N))


def _ffn_kernel(x_ref, mod_ref, g_ref, wg_ref, wu_ref, wd_ref, o_ref, h_ref, *, mod_base):
    j = pl.program_id(1)
    res_gate = 0.5 * mod_ref[0, mod_base + 2:mod_base + 3, :]

    def gated_chunk(h):
        gate = jnp.dot(h, wg_ref[...], preferred_element_type=F32)
        up = jnp.dot(h, wu_ref[...], preferred_element_type=F32)
        act = (gate * jax.nn.sigmoid(gate) * up).astype(BF16)
        return res_gate * jnp.dot(act, wd_ref[...], preferred_element_type=F32)

    @pl.when(j == 0)
    def _():
        shift = mod_ref[0, mod_base:mod_base + 1, :]
        scale = mod_ref[0, mod_base + 1:mod_base + 2, :]
        piece = min(FIRST_STEP_ROWS, x_ref.shape[0])
        for r in range(0, x_ref.shape[0], piece):
            x = x_ref[r:r + piece, :]
            h = (_rms(x, g_ref[...]) * (1.0 + scale) + shift).astype(BF16)
            h_ref[r:r + piece, :] = h
            o_ref[r:r + piece, :] = x + gated_chunk(h)

    @pl.when(j > 0)
    def _():
        o_ref[...] += gated_chunk(h_ref[...])


def _ffn(x, mod, gain, wg, wu, wd, layer, half, mod_base, seq_len):
    T, D = x.shape
    F = wg.shape[-1]
    tm = _pick_tile(seq_len, (1024, 512, 256, 128))
    tf = _pick_tile(F, (512, 256, 128))
    tps = seq_len // tm
    return pl.pallas_call(
        functools.partial(_ffn_kernel, mod_base=mod_base),
        grid=(T // tm, F // tf),
        in_specs=[
            pl.BlockSpec((tm, D), lambda i, j: (i, 0)),
            pl.BlockSpec((1, N_MOD, D), lambda i, j: (i // tps, 0, 0)),
            pl.BlockSpec((1, D), lambda i, j: (0, 0)),
            pl.BlockSpec((None, None, D, tf), lambda i, j: (layer, half, 0, j)),
            pl.BlockSpec((None, None, D, tf), lambda i, j: (layer, half, 0, j)),
            pl.BlockSpec((None, None, tf, D), lambda i, j: (layer, half, j, 0)),
        ],
        out_specs=pl.BlockSpec((tm, D), lambda i, j: (i, 0)),
        out_shape=jax.ShapeDtypeStruct((T, D), F32),
        scratch_shapes=[pltpu.VMEM((tm, D), BF16)],
        compiler_params=pltpu.CompilerParams(
            dimension_semantics=("parallel", "arbitrary"),
            vmem_limit_bytes=_vmem_limit(
                2 * tm * D * 4 + 3 * D * tf * 2 + 16 * D * 4, tm * D * 2, 3 * tm * tf * 4),
        ),
        name="ffn",
    )(x, mod, gain, wg, wu, wd)


def _mix_in_kernel(x_ref, mod_ref, g_ref, win_ref, qg_ref, kvg_ref, wuqt_ref, wuk_ref, wuvt_ref, qtab_ref, ktab_ref,
                   q_ref, k_ref, v_ref, u_ref):
    shift = mod_ref[0, 3:4, :]
    scale = mod_ref[0, 4:5, :]
    h = (_rms(x_ref[...], g_ref[...]) * (1.0 + scale) + shift).astype(BF16)
    z = jnp.dot(h, win_ref[...], preferred_element_type=F32)
    c_kv = Q_LORA_RANK
    c_kr = c_kv + KV_LORA_RANK
    c_u = c_kr + 2 * QK_ROPE_DIM
    u_ref[...] = z[:, c_u:]

    qn = _rms(z[:, :c_kv], qg_ref[...]).astype(BF16)
    kvn = _rms(z[:, c_kv:c_kr], kvg_ref[...]).astype(BF16)

    q_t = lax.dot_general(wuqt_ref[...], qn, NT_DIMS, preferred_element_type=F32)
    qtab = qtab_ref[...]
    for hd in range(N_HEADS):
        rows = slice(hd * HEAD_QK, (hd + 1) * HEAD_QK)
        q_ref[0, 0, rows, :] = (q_t[rows, :] * qtab).astype(BF16)
    v_t = lax.dot_general(wuvt_ref[...], kvn, NT_DIMS, preferred_element_type=F32)
    v_ref[0, 0] = v_t.astype(BF16)

    t = z[:, c_kr:c_u] * ktab_ref[...]
    k_rot = (t + pltpu.roll(t, QK_ROPE_DIM, axis=1)).astype(BF16)
    k_nope = jnp.dot(kvn, wuk_ref[...], preferred_element_type=F32)
    for hd in range(N_HEADS):
        c0 = hd * HEAD_QK
        k_ref[:, c0:c0 + QK_NOPE_DIM] = k_nope[:, hd * QK_NOPE_DIM:(hd + 1) * QK_NOPE_DIM].astype(BF16)
        k_ref[:, c0 + QK_NOPE_DIM:c0 + HEAD_QK] = k_rot


def _mix_in(x, mod, gain, win, qg, kvg, wuqt, wuk, wuvt, qtab_t, ktab, layer, batch, seq_len):
    T, D = x.shape
    n_in = win.shape[-1]
    tm = _seq_tile(seq_len)
    tps = seq_len // tm
    whole = lambda shape: pl.BlockSpec(shape, lambda i: (0,) * len(shape))
    layer_mat = lambda r, c: pl.BlockSpec((None, r, c), lambda i: (layer, 0, 0))
    block_bytes = (tm * D * 4 + D * n_in * 2 + Q_LORA_RANK * QK_WIDTH * 2 + KV_LORA_RANK * 2 * ATTN_WIDTH * 2
                   + tm * (HEAD_QK + 2 * QK_ROPE_DIM) * 4 + 2 * tm * QK_WIDTH * 2 + tm * ATTN_WIDTH * 2
                   + tm * POOL_WIDTH * 4 + 16 * D * 4)
    return pl.pallas_call(
        _mix_in_kernel,
        grid=(T // tm,),
        in_specs=[
            pl.BlockSpec((tm, D), lambda i: (i, 0)),
            pl.BlockSpec((1, N_MOD, D), lambda i: (i // tps, 0, 0)),
            whole((1, D)),
            layer_mat(D, n_in),
            whole((1, Q_LORA_RANK)),
            whole((1, KV_LORA_RANK)),
            layer_mat(QK_WIDTH, Q_LORA_RANK),
            layer_mat(KV_LORA_RANK, N_HEADS * QK_NOPE_DIM),
            layer_mat(ATTN_WIDTH, KV_LORA_RANK),
            pl.BlockSpec((HEAD_QK, tm), lambda i: (0, i % tps)),
            pl.BlockSpec((tm, 2 * QK_ROPE_DIM), lambda i: (i % tps, 0)),
        ],
        out_specs=[
            pl.BlockSpec((1, 1, QK_WIDTH, tm), lambda i: (i // tps, i % tps, 0, 0)),
            pl.BlockSpec((tm, QK_WIDTH), lambda i: (i, 0)),
            pl.BlockSpec((1, 1, ATTN_WIDTH, tm), lambda i: (i // tps, i % tps, 0, 0)),
            pl.BlockSpec((tm, POOL_WIDTH), lambda i: (i, 0)),
        ],
        out_shape=[
            jax.ShapeDtypeStruct((batch, tps, QK_WIDTH, tm), BF16),
            jax.ShapeDtypeStruct((T, QK_WIDTH), BF16),
            jax.ShapeDtypeStruct((batch, tps, ATTN_WIDTH, tm), BF16),
            jax.ShapeDtypeStruct((T, POOL_WIDTH), F32),
        ],
        compiler_params=pltpu.CompilerParams(
            dimension_semantics=("parallel",),
            vmem_limit_bytes=_vmem_limit(block_bytes, 0, tm * (n_in + 2 * QK_WIDTH) * 4 + tm * D * 4),
        ),
        name="mix_in",
    )(x, mod, gain, win, qg, kvg, wuqt, wuk, wuvt, qtab_t, ktab)


def _attn_kernel(q_ref, k_ref, v_ref, o_ref, s0_ref, s1_ref):
    n_t, tq = q_ref.shape[1], q_ref.shape[3]

    def step(t, s_cur, s_prev, m_prev):
        q_t = None if s_cur is None else q_ref[0, t]
        m = l = acc = None
        for c in range(n_t):
            rows = slice(c * tq, (c + 1) * tq)
            if s_cur is not None:
                s = jnp.dot(k_ref[0, rows, :], q_t, preferred_element_type=F32)
                s_cur[rows, :] = s
                mc = jnp.max(s, axis=0, keepdims=True)
                m = mc if m is None else jnp.maximum(m, mc)
            if s_prev is not None:
                p = jnp.exp2(s_prev[rows, :] - m_prev)
                pv = jnp.dot(v_ref[0, c], p.astype(BF16), preferred_element_type=F32)
                lc = jnp.sum(p, axis=0, keepdims=True)
                acc = pv if acc is None else acc + pv
                l = lc if l is None else l + lc
        if s_prev is not None:
            start = (t - 1) * tq
            out_rows = pl.ds(start if isinstance(start, int) else pl.multiple_of(start, tq), tq)
            o_ref[0, out_rows, :] = (acc / l).T
        return m

    m = step(0, s0_ref, None, None)

    def pair(i, m):
        m = step(2 * i + 1, s1_ref, s0_ref, m)
        return step(2 * i + 2, s0_ref, s1_ref, m)

    m = lax.fori_loop(0, (n_t - 1) // 2, pair, m)
    if n_t % 2 == 0:
        m = step(n_t - 1, s1_ref, s0_ref, m)
    step(n_t, None, s1_ref if n_t % 2 == 0 else s0_ref, m)


def _attention(q_t, k, v_t):
    B, n_t, _, tq = q_t.shape
    S = k.shape[1]
    block_bytes = S * HEAD_QK * 2 * 2 + S * V_HEAD_DIM * 2 + S * V_HEAD_DIM * 4
    scratch_bytes = 2 * S * tq * 4
    return pl.pallas_call(
        _attn_kernel,
        grid=(B, N_HEADS),
        in_specs=[
            pl.BlockSpec((1, n_t, HEAD_QK, tq), lambda b, h: (b, 0, h, 0)),
            pl.BlockSpec((1, S, HEAD_QK), lambda b, h: (b, 0, h)),
            pl.BlockSpec((1, n_t, V_HEAD_DIM, tq), lambda b, h: (b, 0, h, 0)),
        ],
        out_specs=pl.BlockSpec((1, S, V_HEAD_DIM), lambda b, h: (b, 0, h)),
        out_shape=jax.ShapeDtypeStruct((B, S, N_HEADS * V_HEAD_DIM), F32),
        scratch_shapes=[pltpu.VMEM((S, tq), F32), pltpu.VMEM((S, tq), F32)],
        compiler_params=pltpu.CompilerParams(
            dimension_semantics=("parallel", "parallel"),
            vmem_limit_bytes=_vmem_limit(block_bytes, scratch_bytes, 6 * tq * tq * 4),
        ),
        name="attn",
    )(q_t, k, v_t)


def _mix_out_kernel(x_ref, mod_ref, ao_ref, aog_ref, u_ref, up_ref, un_ref, band_ref, wpool_ref, ps_ref, wout_ref,
                    o_ref, ext_ref, cat_ref, *, tps, seq_len):
    tm = x_ref.shape[0]
    si = pl.program_id(0) % tps
    cat_ref[:, :ATTN_WIDTH] = _rms(ao_ref[...], aog_ref[...]).astype(BF16)

    u = u_ref[...]
    ext_ref[0:HALO, :] = jnp.where(si == 0, 0.0, up_ref[...]).astype(BF16)
    ext_ref[HALO:HALO + tm, :] = u.astype(BF16)
    ext_ref[HALO + tm:2 * HALO + tm, :] = jnp.where(si == tps - 1, 0.0, un_ref[...]).astype(BF16)
    ext_ref[2 * HALO + tm:, :] = jnp.zeros((BAND_ROWS - 2 * HALO, POOL_WIDTH), BF16)
    pos = si * tm + lax.broadcasted_iota(jnp.int32, (tm, 1), 0)
    for g, w in enumerate(POOL_WINDOWS):
        c0 = g * POOL_GROUP_WIDTH
        cols = slice(c0, c0 + POOL_GROUP_WIDTH)
        win_sum = jnp.concatenate(
            [jnp.dot(band_ref[g], ext_ref[r:r + 2 * BAND_ROWS, cols], preferred_element_type=F32)
             for r in range(0, tm, BAND_ROWS)], axis=0)
        lo = jnp.maximum(pos - (w // 2 - 1), 0)
        hi = jnp.minimum(pos + w // 2 + 1, seq_len)
        pooled = win_sum / (hi - lo).astype(F32) - u[:, cols]
        y = jnp.dot(pooled.astype(BF16), wpool_ref[g], preferred_element_type=F32) * ps_ref[:, cols]
        cat_ref[:, ATTN_WIDTH + c0:ATTN_WIDTH + c0 + POOL_GROUP_WIDTH] = y.astype(BF16)

    out = jnp.dot(cat_ref[...], wout_ref[...], preferred_element_type=F32)
    o_ref[...] = x_ref[...] + mod_ref[0, 5:6, :] * out


def _pool_bands():
    t = jnp.arange(BAND_ROWS)[:, None]
    s = jnp.arange(2 * BAND_ROWS)[None, :] - HALO
    return jnp.stack([(s >= t - (w // 2 - 1)) & (s <= t + w // 2) for w in POOL_WINDOWS]).astype(BF16)


def _mix_out(x, mod, ao, aog, u, wpool, ps, wout, layer, seq_len):
    T, D = x.shape
    tm = _pick_tile(seq_len, (512, 256))
    tps = seq_len // tm
    assert tm % BAND_ROWS == 0 and max(POOL_WINDOWS) // 2 <= HALO
    halo_blocks = tm // HALO
    n_halo = T // HALO
    n_win = len(POOL_WINDOWS)
    ext_rows = tm + BAND_ROWS
    block_bytes = (2 * tm * D * 4 + tm * ATTN_WIDTH * 4 + (tm + 2 * HALO) * POOL_WIDTH * 4
                   + n_win * BAND_ROWS * 2 * BAND_ROWS * 2 + n_win * POOL_GROUP_WIDTH ** 2 * 2
                   + (ATTN_WIDTH + POOL_WIDTH) * D * 2 + 32 * D * 4)
    return pl.pallas_call(
        functools.partial(_mix_out_kernel, tps=tps, seq_len=seq_len),
        grid=(T // tm,),
        in_specs=[
            pl.BlockSpec((tm, D), lambda i: (i, 0)),
            pl.BlockSpec((1, N_MOD, D), lambda i: (i // tps, 0, 0)),
            pl.BlockSpec((tm, ATTN_WIDTH), lambda i: (i, 0)),
            pl.BlockSpec((1, ATTN_WIDTH), lambda i: (0, 0)),
            pl.BlockSpec((tm, POOL_WIDTH), lambda i: (i, 0)),
            pl.BlockSpec((HALO, POOL_WIDTH), lambda i: (jnp.maximum(i * halo_blocks - 1, 0), 0)),
            pl.BlockSpec((HALO, POOL_WIDTH), lambda i: (jnp.minimum((i + 1) * halo_blocks, n_halo - 1), 0)),
            pl.BlockSpec((n_win, BAND_ROWS, 2 * BAND_ROWS), lambda i: (0, 0, 0)),
            pl.BlockSpec((None, n_win, POOL_GROUP_WIDTH, POOL_GROUP_WIDTH), lambda i: (layer, 0, 0, 0)),
            pl.BlockSpec((1, POOL_WIDTH), lambda i: (0, 0)),
            pl.BlockSpec((None, ATTN_WIDTH + POOL_WIDTH, D), lambda i: (layer, 0, 0)),
        ],
        out_specs=pl.BlockSpec((tm, D), lambda i: (i, 0)),
        out_shape=jax.ShapeDtypeStruct((T, D), F32),
        scratch_shapes=[
            pltpu.VMEM((ext_rows, POOL_WIDTH), BF16),
            pltpu.VMEM((tm, ATTN_WIDTH + POOL_WIDTH), BF16),
        ],
        compiler_params=pltpu.CompilerParams(
            dimension_semantics=("parallel",),
            vmem_limit_bytes=_vmem_limit(block_bytes, ext_rows * POOL_WIDTH * 2 + tm * D * 2, 2 * tm * D * 4),
        ),
        name="mix_out",
    )(x, mod, ao, aog, u, u, u, _pool_bands(), wpool, ps, wout)


def _final_kernel(x_ref, mod_ref, g_ref, o_ref):
    o_ref[...] = _rms(x_ref[...], g_ref[...]) * (1.0 + mod_ref[0, 1:2, :]) + mod_ref[0, 0:1, :]


def _final_norm(x, mod, gain, seq_len):
    T, D = x.shape
    tm = _pick_tile(seq_len, (512, 256, 128))
    tps = seq_len // tm
    return pl.pallas_call(
        _final_kernel,
        grid=(T // tm,),
        in_specs=[
            pl.BlockSpec((tm, D), lambda i: (i, 0)),
            pl.BlockSpec((1, 2, D), lambda i: (i // tps, 0, 0)),
            pl.BlockSpec((1, D), lambda i: (0, 0)),
        ],
        out_specs=pl.BlockSpec((tm, D), lambda i: (i, 0)),
        out_shape=jax.ShapeDtypeStruct((T, D), F32),
        compiler_params=pltpu.CompilerParams(
            dimension_semantics=("parallel",),
            vmem_limit_bytes=_vmem_limit(2 * tm * D * 4 + 16 * D * 4, 0, tm * D * 4),
        ),
        name="final_norm",
    )(x, mod, gain)


def _rot_half_cols(w):
    half = w.shape[-1] // 2
    return jnp.concatenate([-w[..., half:], w[..., :half]], axis=-1)


def _rope_tables(seq_len):
    inv = 1.0 / (ROPE_BASE ** (jnp.arange(0, QK_ROPE_DIM, 2, dtype=F32) / QK_ROPE_DIM))
    ang = jnp.arange(seq_len, dtype=F32)[:, None] * inv[None, :]
    cos, sin = jnp.cos(ang), jnp.sin(ang)
    ktab = jnp.concatenate([cos, cos, sin, sin], axis=-1)
    qscale = (QK_NOPE_DIM + QK_ROPE_DIM) ** -0.5 * LOG2E
    qtab = qscale * jnp.concatenate([jnp.ones((seq_len, QK_NOPE_DIM), F32), ktab], axis=-1)
    return qtab.T, ktab


def _trunk(x, mod, mod_final, p):
    B, S, D = x.shape
    L = p["norm_g"].shape[0]
    qtab_t, ktab = _rope_tables(S)
    xf = x.reshape(B * S, D)
    for l in range(L):
        ml = mod[l]
        xf = _ffn(xf, ml, p["norm_g"][l, 0][None], p["wg"], p["wu"], p["wd"], l, 0, 0, S)
        q_t, k, v_t, u = _mix_in(xf, ml, p["norm_g"][l, 1][None], p["win"], p["q_norm_g"][l][None],
                                 p["kv_norm_g"][l][None], p["wuqt"], p["wuk"], p["wuvt"], qtab_t, ktab, l, B, S)
        ao = _attention(q_t, k.reshape(B, S, QK_WIDTH), v_t)
        xf = _mix_out(xf, ml, ao.reshape(B * S, ATTN_WIDTH), p["attn_out_g"][l][None], u, p["wpool"],
                      p["pool_scale"][l][None], p["wout"], l, S)
        xf = _ffn(xf, ml, p["norm_g"][l, 2][None], p["wg"], p["wu"], p["wd"], l, 1, 6, S)
    return _final_norm(xf, mod_final, p["final_g"][None], S).reshape(B, S, D)


def kernel(x_prompt, x_sample, c_prompt, c_sample, norm_g, w_mod, b_mod, ffn_w_gate, ffn_w_up, ffn_w_down, w_in, q_norm_g, kv_norm_g, w_uq, w_ukv, attn_out_g, w_pool, pool_scale, w_out, w_mod_final, b_mod_final, final_g):
    L, D = norm_g.shape[0], norm_g.shape[-1]
    bp, bs = x_prompt.shape[0], x_sample.shape[0]

    c_all = jnp.concatenate([c_prompt, c_sample], axis=0)
    mod_all = _modulation(c_all, w_mod, b_mod).reshape(L, bp + bs, N_MOD, D)
    modf_all = _modulation(c_all, w_mod_final[None], b_mod_final[None]).reshape(bp + bs, 2, D)

    c_kr = Q_LORA_RANK + KV_LORA_RANK
    c_u = c_kr + QK_ROPE_DIM
    w_kr = w_in[:, :, c_kr:c_u]
    win = jnp.concatenate([w_in[:, :, :c_u], _rot_half_cols(w_kr), w_in[:, :, c_u:]], axis=-1).astype(BF16)
    wq = w_uq.reshape(L, Q_LORA_RANK, N_HEADS, QK_NOPE_DIM + QK_ROPE_DIM)
    wq = jnp.concatenate([wq, _rot_half_cols(wq[..., QK_NOPE_DIM:])], axis=-1)
    wuqt = jnp.swapaxes(wq.reshape(L, Q_LORA_RANK, QK_WIDTH), 1, 2).astype(BF16)
    wkv = w_ukv.reshape(L, KV_LORA_RANK, N_HEADS, QK_NOPE_DIM + V_HEAD_DIM)
    wuk = wkv[..., :QK_NOPE_DIM].reshape(L, KV_LORA_RANK, N_HEADS * QK_NOPE_DIM).astype(BF16)
    wuvt = jnp.swapaxes(wkv[..., QK_NOPE_DIM:].reshape(L, KV_LORA_RANK, ATTN_WIDTH), 1, 2).astype(BF16)
    p = {
        "norm_g": norm_g, "q_norm_g": q_norm_g, "kv_norm_g": kv_norm_g, "attn_out_g": attn_out_g,
        "pool_scale": pool_scale, "final_g": final_g,
        "wg": ffn_w_gate.astype(BF16), "wu": ffn_w_up.astype(BF16), "wd": ffn_w_down.astype(BF16),
        "win": win, "wuqt": wuqt, "wuk": wuk, "wuvt": wuvt,
        "wpool": w_pool.astype(BF16), "wout": w_out.astype(BF16),
    }
    y_prompt = _trunk(x_prompt, mod_all[:, :bp], modf_all[:bp], p)
    y_sample = _trunk(x_sample, mod_all[:, bp:], modf_all[bp:], p)
    return (y_prompt, y_sample)
```

```python
import functools
import math

import jax
import jax.numpy as jnp
from jax import lax
from jax.experimental import pallas as pl
from jax.experimental.pallas import tpu as pltpu

F32 = jnp.float32
BF16 = jnp.bfloat16

N_HEADS = 8
QK_NOPE_DIM = 128
QK_ROPE_DIM = 64
V_HEAD_DIM = 128
Q_LORA_RANK = 512
KV_LORA_RANK = 512
ROPE_BASE = 10000.0
POOL_WINDOWS = (2, 4, 8, 16)
POOL_GROUP_WIDTH = 256
POOL_WIDTH = POOL_GROUP_WIDTH * len(POOL_WINDOWS)
ATTN_WIDTH = N_HEADS * V_HEAD_DIM
N_MOD = 9
NORM_EPS = 1e-6
HEAD_QK = QK_NOPE_DIM + 2 * QK_ROPE_DIM
QK_WIDTH = N_HEADS * HEAD_QK
FIRST_STEP_ROWS = 256
HALO = 16
BAND_ROWS = 256
LOG2E = math.log2(math.e)
NT_DIMS = (((1,), (1,)), ((), ()))

VMEM_LIMIT_CAP = 60 * 1024 * 1024


def _vmem_limit(block_bytes, scratch_bytes, temp_bytes):
    return int(min(VMEM_LIMIT_CAP, 2 * block_bytes + scratch_bytes + temp_bytes + (4 << 20)))


def _pick_tile(n, preferred):
    for t in preferred:
        if n % t == 0:
            return t
    return n


def _seq_tile(seq_len):
    return _pick_tile(seq_len, (512, 256, 128))


def _rms(x, gain):
    ms = jnp.mean(x * x, axis=-1, keepdims=True)
    return x * lax.rsqrt(ms + NORM_EPS) * gain


def _mod_kernel(c_ref, w_ref, b_ref, o_ref):
    c = c_ref[...]
    cond = (c * jax.nn.sigmoid(c)).astype(BF16)
    o_ref[...] = jnp.dot(cond, w_ref[...].astype(BF16), preferred_element_type=F32) + b_ref[...]


def _modulation(c, w, b):
    L, D, N = w.shape
    R = c.shape[0]
    tn = _pick_tile(N, (1024, 512, 256, 128))
    return pl.pallas_call(
        _mod_kernel,
        grid=(L, N // tn),
        in_specs=[
            pl.BlockSpec((R, D), lambda l, j: (0, 0)),
            pl.BlockSpec((None, D, tn), lambda l, j: (l, 0, j)),
            pl.BlockSpec((None, 1, tn), lambda l, j: (l, 0, j)),
        ],
        out_specs=pl.BlockSpec((None, R, tn), lambda l, j: (l, 0, j)),
        out_shape=jax.ShapeDtypeStruct((L, R, N), F32),
        compiler_params=pltpu.CompilerParams(
            dimension_semantics=("parallel", "parallel"),
            vmem_limit_bytes=_vmem_limit(D * tn * 4 + R * D * 4 + R * tn * 4, 0, D * tn * 2),
        ),
        name="adaln_mod",
    )(c, w, b.reshape(L, 1, N))


def _ffn_kernel(x_ref, mod_ref, g_ref, wg_ref, wu_ref, wd_ref, modf_ref, gf_ref, o_ref, h_ref, *, mod_base, final):
    j = pl.program_id(1)
    last = pl.num_programs(1) - 1
    piece = min(FIRST_STEP_ROWS, x_ref.shape[0])
    res_gate = 0.5 * mod_ref[0, mod_base + 2:mod_base + 3, :]

    def gated_chunk(h):
        gate = jnp.dot(h, wg_ref[...], preferred_element_type=F32)
        up = jnp.dot(h, wu_ref[...], preferred_element_type=F32)
        act = (gate * jax.nn.sigmoid(gate) * up).astype(BF16)
        return res_gate * jnp.dot(act, wd_ref[...], preferred_element_type=F32)

    @pl.when(j == 0)
    def _():
        shift = mod_ref[0, mod_base:mod_base + 1, :]
        scale = mod_ref[0, mod_base + 1:mod_base + 2, :]
        for r in range(0, x_ref.shape[0], piece):
            x = x_ref[r:r + piece, :]
            h = (_rms(x, g_ref[...]) * (1.0 + scale) + shift).astype(BF16)
            h_ref[r:r + piece, :] = h
            o_ref[r:r + piece, :] = x + gated_chunk(h)

    @pl.when((j > 0) & (j < last) if final else j > 0)
    def _():
        o_ref[...] += gated_chunk(h_ref[...])

    if final:
        @pl.when(j == last)
        def _():
            for r in range(0, x_ref.shape[0], piece):
                y = o_ref[r:r + piece, :] + gated_chunk(h_ref[r:r + piece, :])
                o_ref[r:r + piece, :] = _rms(y, gf_ref[...]) * (1.0 + modf_ref[0, 1:2, :]) + modf_ref[0, 0:1, :]


def _ffn(x, mod, gain, wg, wu, wd, mod_final, gain_final, layer, half, mod_base, seq_len, final):
    T, D = x.shape
    F = wg.shape[-1]
    tm = _pick_tile(seq_len, (1024, 512, 256, 128))
    tf = _pick_tile(F, (512, 256, 128))
    tps = seq_len // tm
    assert F // tf >= 2
    return pl.pallas_call(
        functools.partial(_ffn_kernel, mod_base=mod_base, final=final),
        grid=(T // tm, F // tf),
        in_specs=[
            pl.BlockSpec((tm, D), lambda i, j: (i, 0)),
            pl.BlockSpec((1, N_MOD, D), lambda i, j: (i // tps, 0, 0)),
            pl.BlockSpec((1, D), lambda i, j: (0, 0)),
            pl.BlockSpec((None, None, D, tf), lambda i, j: (layer, half, 0, j)),
            pl.BlockSpec((None, None, D, tf), lambda i, j: (layer, half, 0, j)),
            pl.BlockSpec((None, None, tf, D), lambda i, j: (layer, half, j, 0)),
            pl.BlockSpec((1, 2, D), lambda i, j: (i // tps, 0, 0)),
            pl.BlockSpec((1, D), lambda i, j: (0, 0)),
        ],
        out_specs=pl.BlockSpec((tm, D), lambda i, j: (i, 0)),
        out_shape=jax.ShapeDtypeStruct((T, D), F32),
        scratch_shapes=[pltpu.VMEM((tm, D), BF16)],
        compiler_params=pltpu.CompilerParams(
            dimension_semantics=("parallel", "arbitrary"),
            vmem_limit_bytes=_vmem_limit(
                2 * tm * D * 4 + 3 * D * tf * 2 + 16 * D * 4, tm * D * 2, 3 * tm * tf * 4),
        ),
        name="ffn",
    )(x, mod, gain, wg, wu, wd, mod_final, gain_final)


def _mix_in_kernel(x_ref, mod_ref, g_ref, win_ref, qg_ref, kvg_ref, wuqt_ref, wuk_ref, wuvt_ref, qtab_ref, ktab_ref,
                   q_ref, k_ref, v_ref, u_ref):
    shift = mod_ref[0, 3:4, :]
    scale = mod_ref[0, 4:5, :]
    h = (_rms(x_ref[...], g_ref[...]) * (1.0 + scale) + shift).astype(BF16)
    z = jnp.dot(h, win_ref[...], preferred_element_type=F32)
    c_kv = Q_LORA_RANK
    c_kr = c_kv + KV_LORA_RANK
    c_u = c_kr + 2 * QK_ROPE_DIM
    u_ref[...] = z[:, c_u:]

    qn = _rms(z[:, :c_kv], qg_ref[...]).astype(BF16)
    kvn = _rms(z[:, c_kv:c_kr], kvg_ref[...]).astype(BF16)

    q_t = lax.dot_general(wuqt_ref[...], qn, NT_DIMS, preferred_element_type=F32)
    qtab = qtab_ref[...]
    for hd in range(N_HEADS):
        rows = slice(hd * HEAD_QK, (hd + 1) * HEAD_QK)
        q_ref[0, 0, rows, :] = (q_t[rows, :] * qtab).astype(BF16)
    v_t = lax.dot_general(wuvt_ref[...], kvn, NT_DIMS, preferred_element_type=F32)
    v_ref[0, 0] = v_t.astype(BF16)

    t = z[:, c_kr:c_u] * ktab_ref[...]
    k_rot = (t + pltpu.roll(t, QK_ROPE_DIM, axis=1)).astype(BF16)
    k_nope = jnp.dot(kvn, wuk_ref[...], preferred_element_type=F32)
    for hd in range(N_HEADS):
        c0 = hd * HEAD_QK
        k_ref[:, c0:c0 + QK_NOPE_DIM] = k_nope[:, hd * QK_NOPE_DIM:(hd + 1) * QK_NOPE_DIM].astype(BF16)
        k_ref[:, c0 + QK_NOPE_DIM:c0 + HEAD_QK] = k_rot


def _mix_in(x, mod, gain, win, qg, kvg, wuqt, wuk, wuvt, qtab_t, ktab, layer, batch, seq_len):
    T, D = x.shape
    n_in = win.shape[-1]
    tm = _seq_tile(seq_len)
    tps = seq_len // tm
    whole = lambda shape: pl.BlockSpec(shape, lambda i: (0,) * len(shape))
    layer_mat = lambda r, c: pl.BlockSpec((None, r, c), lambda i: (layer, 0, 0))
    block_bytes = (tm * D * 4 + D * n_in * 2 + Q_LORA_RANK * QK_WIDTH * 2 + KV_LORA_RANK * 2 * ATTN_WIDTH * 2
                   + tm * (HEAD_QK + 2 * QK_ROPE_DIM) * 4 + 2 * tm * QK_WIDTH * 2 + tm * ATTN_WIDTH * 2
                   + tm * POOL_WIDTH * 4 + 16 * D * 4)
    return pl.pallas_call(
        _mix_in_kernel,
        grid=(T // tm,),
        in_specs=[
            pl.BlockSpec((tm, D), lambda i: (i, 0)),
            pl.BlockSpec((1, N_MOD, D), lambda i: (i // tps, 0, 0)),
            whole((1, D)),
            layer_mat(D, n_in),
            whole((1, Q_LORA_RANK)),
            whole((1, KV_LORA_RANK)),
            layer_mat(QK_WIDTH, Q_LORA_RANK),
            layer_mat(KV_LORA_RANK, N_HEADS * QK_NOPE_DIM),
            layer_mat(ATTN_WIDTH, KV_LORA_RANK),
            pl.BlockSpec((HEAD_QK, tm), lambda i: (0, i % tps)),
            pl.BlockSpec((tm, 2 * QK_ROPE_DIM), lambda i: (i % tps, 0)),
        ],
        out_specs=[
            pl.BlockSpec((1, 1, QK_WIDTH, tm), lambda i: (i // tps, i % tps, 0, 0)),
            pl.BlockSpec((tm, QK_WIDTH), lambda i: (i, 0)),
            pl.BlockSpec((1, 1, ATTN_WIDTH, tm), lambda i: (i // tps, i % tps, 0, 0)),
            pl.BlockSpec((tm, POOL_WIDTH), lambda i: (i, 0)),
        ],
        out_shape=[
            jax.ShapeDtypeStruct((batch, tps, QK_WIDTH, tm), BF16),
            jax.ShapeDtypeStruct((T, QK_WIDTH), BF16),
            jax.ShapeDtypeStruct((batch, tps, ATTN_WIDTH, tm), BF16),
            jax.ShapeDtypeStruct((T, POOL_WIDTH), F32),
        ],
        compiler_params=pltpu.CompilerParams(
            dimension_semantics=("parallel",),
            vmem_limit_bytes=_vmem_limit(block_bytes, 0, tm * (n_in + 2 * QK_WIDTH) * 4 + tm * D * 4),
        ),
        name="mix_in",
    )(x, mod, gain, win, qg, kvg, wuqt, wuk, wuvt, qtab_t, ktab)


def _attn_kernel(q_ref, k_ref, v_ref, o_ref, s0_ref, s1_ref):
    n_t, tq = q_ref.shape[1], q_ref.shape[3]

    def step(t, s_cur, s_prev, m_prev):
        q_t = None if s_cur is None else q_ref[0, t]
        m = l = acc = None
        for c in range(n_t):
            rows = slice(c * tq, (c + 1) * tq)
            if s_cur is not None:
                s = jnp.dot(k_ref[0, rows, :], q_t, preferred_element_type=F32)
                s_cur[rows, :] = s
                mc = jnp.max(s, axis=0, keepdims=True)
                m = mc if m is None else jnp.maximum(m, mc)
            if s_prev is not None:
                p = jnp.exp2(s_prev[rows, :] - m_prev)
                pv = jnp.dot(v_ref[0, c], p.astype(BF16), preferred_element_type=F32)
                lc = jnp.sum(p, axis=0, keepdims=True)
                acc = pv if acc is None else acc + pv
                l = lc if l is None else l + lc
        if s_prev is not None:
            start = (t - 1) * tq
            out_rows = pl.ds(start if isinstance(start, int) else pl.multiple_of(start, tq), tq)
            o_ref[0, out_rows, :] = (acc / l).T
        return m

    m = step(0, s0_ref, None, None)

    def pair(i, m):
        m = step(2 * i + 1, s1_ref, s0_ref, m)
        return step(2 * i + 2, s0_ref, s1_ref, m)

    m = lax.fori_loop(0, (n_t - 1) // 2, pair, m)
    if n_t % 2 == 0:
        m = step(n_t - 1, s1_ref, s0_ref, m)
    step(n_t, None, s1_ref if n_t % 2 == 0 else s0_ref, m)


def _attention(q_t, k, v_t):
    B, n_t, _, tq = q_t.shape
    S = k.shape[1]
    block_bytes = S * HEAD_QK * 2 * 2 + S * V_HEAD_DIM * 2 + S * V_HEAD_DIM * 4
    scratch_bytes = 2 * S * tq * 4
    return pl.pallas_call(
        _attn_kernel,
        grid=(B, N_HEADS),
        in_specs=[
            pl.BlockSpec((1, n_t, HEAD_QK, tq), lambda b, h: (b, 0, h, 0)),
            pl.BlockSpec((1, S, HEAD_QK), lambda b, h: (b, 0, h)),
            pl.BlockSpec((1, n_t, V_HEAD_DIM, tq), lambda b, h: (b, 0, h, 0)),
        ],
        out_specs=pl.BlockSpec((1, S, V_HEAD_DIM), lambda b, h: (b, 0, h)),
        out_shape=jax.ShapeDtypeStruct((B, S, N_HEADS * V_HEAD_DIM), F32),
        scratch_shapes=[pltpu.VMEM((S, tq), F32), pltpu.VMEM((S, tq), F32)],
        compiler_params=pltpu.CompilerParams(
            dimension_semantics=("parallel", "parallel"),
            vmem_limit_bytes=_vmem_limit(block_bytes, scratch_bytes, 6 * tq * tq * 4),
        ),
        name="attn",
    )(q_t, k, v_t)


def _mix_out_kernel(x_ref, mod_ref, ao_ref, aog_ref, u_ref, up_ref, un_ref, band_ref, wpool_ref, ps_ref, wout_ref,
                    o_ref, ext_ref, cat_ref, *, tps, seq_len):
    tm = x_ref.shape[0]
    si = pl.program_id(0) % tps
    cat_ref[:, :ATTN_WIDTH] = _rms(ao_ref[...], aog_ref[...]).astype(BF16)

    u = u_ref[...]
    ext_ref[0:HALO, :] = jnp.where(si == 0, 0.0, up_ref[...]).astype(BF16)
    ext_ref[HALO:HALO + tm, :] = u.astype(BF16)
    ext_ref[HALO + tm:2 * HALO + tm, :] = jnp.where(si == tps - 1, 0.0, un_ref[...]).astype(BF16)
    ext_ref[2 * HALO + tm:, :] = jnp.zeros((BAND_ROWS - 2 * HALO, POOL_WIDTH), BF16)
    pos = si * tm + lax.broadcasted_iota(jnp.int32, (tm, 1), 0)
    for g, w in enumerate(POOL_WINDOWS):
        c0 = g * POOL_GROUP_WIDTH
        cols = slice(c0, c0 + POOL_GROUP_WIDTH)
        win_sum = jnp.concatenate(
            [jnp.dot(band_ref[g], ext_ref[r:r + 2 * BAND_ROWS, cols], preferred_element_type=F32)
             for r in range(0, tm, BAND_ROWS)], axis=0)
        lo = jnp.maximum(pos - (w // 2 - 1), 0)
        hi = jnp.minimum(pos + w // 2 + 1, seq_len)
        pooled = win_sum / (hi - lo).astype(F32) - u[:, cols]
        y = jnp.dot(pooled.astype(BF16), wpool_ref[g], preferred_element_type=F32) * ps_ref[:, cols]
        cat_ref[:, ATTN_WIDTH + c0:ATTN_WIDTH + c0 + POOL_GROUP_WIDTH] = y.astype(BF16)

    out = jnp.dot(cat_ref[...], wout_ref[...], preferred_element_type=F32)
    o_ref[...] = x_ref[...] + mod_ref[0, 5:6, :] * out


def _pool_bands():
    t = jnp.arange(BAND_ROWS)[:, None]
    s = jnp.arange(2 * BAND_ROWS)[None, :] - HALO
    return jnp.stack([(s >= t - (w // 2 - 1)) & (s <= t + w // 2) for w in POOL_WINDOWS]).astype(BF16)


def _mix_out(x, mod, ao, aog, u, wpool, ps, wout, layer, seq_len):
    T, D = x.shape
    tm = _pick_tile(seq_len, (512, 256))
    tps = seq_len // tm
    assert tm % BAND_ROWS == 0 and max(POOL_WINDOWS) // 2 <= HALO
    halo_blocks = tm // HALO
    n_halo = T // HALO
    n_win = len(POOL_WINDOWS)
    ext_rows = tm + BAND_ROWS
    block_bytes = (2 * tm * D * 4 + tm * ATTN_WIDTH * 4 + (tm + 2 * HALO) * POOL_WIDTH * 4
                   + n_win * BAND_ROWS * 2 * BAND_ROWS * 2 + n_win * POOL_GROUP_WIDTH ** 2 * 2
                   + (ATTN_WIDTH + POOL_WIDTH) * D * 2 + 32 * D * 4)
    return pl.pallas_call(
        functools.partial(_mix_out_kernel, tps=tps, seq_len=seq_len),
        grid=(T // tm,),
        in_specs=[
            pl.BlockSpec((tm, D), lambda i: (i, 0)),
            pl.BlockSpec((1, N_MOD, D), lambda i: (i // tps, 0, 0)),
            pl.BlockSpec((tm, ATTN_WIDTH), lambda i: (i, 0)),
            pl.BlockSpec((1, ATTN_WIDTH), lambda i: (0, 0)),
            pl.BlockSpec((tm, POOL_WIDTH), lambda i: (i, 0)),
            pl.BlockSpec((HALO, POOL_WIDTH), lambda i: (jnp.maximum(i * halo_blocks - 1, 0), 0)),
            pl.BlockSpec((HALO, POOL_WIDTH), lambda i: (jnp.minimum((i + 1) * halo_blocks, n_halo - 1), 0)),
            pl.BlockSpec((n_win, BAND_ROWS, 2 * BAND_ROWS), lambda i: (0, 0, 0)),
            pl.BlockSpec((None, n_win, POOL_GROUP_WIDTH, POOL_GROUP_WIDTH), lambda i: (layer, 0, 0, 0)),
            pl.BlockSpec((1, POOL_WIDTH), lambda i: (0, 0)),
            pl.BlockSpec((None, ATTN_WIDTH + POOL_WIDTH, D), lambda i: (layer, 0, 0)),
        ],
        out_specs=pl.BlockSpec((tm, D), lambda i: (i, 0)),
        out_shape=jax.ShapeDtypeStruct((T, D), F32),
        scratch_shapes=[
            pltpu.VMEM((ext_rows, POOL_WIDTH), BF16),
            pltpu.VMEM((tm, ATTN_WIDTH + POOL_WIDTH), BF16),
        ],
        compiler_params=pltpu.CompilerParams(
            dimension_semantics=("parallel",),
            vmem_limit_bytes=_vmem_limit(block_bytes, ext_rows * POOL_WIDTH * 2 + tm * D * 2, 2 * tm * D * 4),
        ),
        name="mix_out",
    )(x, mod, ao, aog, u, u, u, _pool_bands(), wpool, ps, wout)


def _rot_half_cols(w):
    half = w.shape[-1] // 2
    return jnp.concatenate([-w[..., half:], w[..., :half]], axis=-1)


def _rope_tables(seq_len):
    inv = 1.0 / (ROPE_BASE ** (jnp.arange(0, QK_ROPE_DIM, 2, dtype=F32) / QK_ROPE_DIM))
    ang = jnp.arange(seq_len, dtype=F32)[:, None] * inv[None, :]
    cos, sin = jnp.cos(ang), jnp.sin(ang)
    ktab = jnp.concatenate([cos, cos, sin, sin], axis=-1)
    qscale = (QK_NOPE_DIM + QK_ROPE_DIM) ** -0.5 * LOG2E
    qtab = qscale * jnp.concatenate([jnp.ones((seq_len, QK_NOPE_DIM), F32), ktab], axis=-1)
    return qtab.T, ktab


def _trunk(x, mod, mod_final, p):
    B, S, D = x.shape
    L = p["norm_g"].shape[0]
    assert L >= 1
    qtab_t, ktab = _rope_tables(S)
    ffn = functools.partial(_ffn, wg=p["wg"], wu=p["wu"], wd=p["wd"], mod_final=mod_final,
                            gain_final=p["final_g"][None], seq_len=S)
    xf = x.reshape(B * S, D)
    for l in range(L):
        ml = mod[l]
        xf = ffn(xf, ml, p["norm_g"][l, 0][None], layer=l, half=0, mod_base=0, final=False)
        q_t, k, v_t, u = _mix_in(xf, ml, p["norm_g"][l, 1][None], p["win"], p["q_norm_g"][l][None],
                                 p["kv_norm_g"][l][None], p["wuqt"], p["wuk"], p["wuvt"], qtab_t, ktab, l, B, S)
        ao = _attention(q_t, k.reshape(B, S, QK_WIDTH), v_t)
        xf = _mix_out(xf, ml, ao.reshape(B * S, ATTN_WIDTH), p["attn_out_g"][l][None], u, p["wpool"],
                      p["pool_scale"][l][None], p["wout"], l, S)
        xf = ffn(xf, ml, p["norm_g"][l, 2][None], layer=l, half=1, mod_base=6, final=(l == L - 1))
    return xf.reshape(B, S, D)


def kernel(x_prompt, x_sample, c_prompt, c_sample, norm_g, w_mod, b_mod, ffn_w_gate, ffn_w_up, ffn_w_down, w_in, q_norm_g, kv_norm_g, w_uq, w_ukv, attn_out_g, w_pool, pool_scale, w_out, w_mod_final, b_mod_final, final_g):
    L, D = norm_g.shape[0], norm_g.shape[-1]
    bp, bs = x_prompt.shape[0], x_sample.shape[0]

    c_all = jnp.concatenate([c_prompt, c_sample], axis=0)
    mod_all = _modulation(c_all, w_mod, b_mod).reshape(L, bp + bs, N_MOD, D)
    modf_all = _modulation(c_all, w_mod_final[None], b_mod_final[None]).reshape(bp + bs, 2, D)

    c_kr = Q_LORA_RANK + KV_LORA_RANK
    c_u = c_kr + QK_ROPE_DIM
    w_in = w_in.astype(BF16)
    win = jnp.concatenate([w_in[:, :, :c_u], _rot_half_cols(w_in[:, :, c_kr:c_u]), w_in[:, :, c_u:]], axis=-1)
    wq = w_uq.astype(BF16).reshape(L, Q_LORA_RANK, N_HEADS, QK_NOPE_DIM + QK_ROPE_DIM)
    wq = jnp.concatenate([wq, _rot_half_cols(wq[..., QK_NOPE_DIM:])], axis=-1)
    wuqt = jnp.swapaxes(wq.reshape(L, Q_LORA_RANK, QK_WIDTH), 1, 2)
    wkv = w_ukv.astype(BF16).reshape(L, KV_LORA_RANK, N_HEADS, QK_NOPE_DIM + V_HEAD_DIM)
    wuk = wkv[..., :QK_NOPE_DIM].reshape(L, KV_LORA_RANK, N_HEADS * QK_NOPE_DIM)
    wuvt = jnp.swapaxes(wkv[..., QK_NOPE_DIM:].reshape(L, KV_LORA_RANK, ATTN_WIDTH), 1, 2)
    p = {
        "norm_g": norm_g, "q_norm_g": q_norm_g, "kv_norm_g": kv_norm_g, "attn_out_g": attn_out_g,
        "pool_scale": pool_scale, "final_g": final_g,
        "wg": ffn_w_gate.astype(BF16), "wu": ffn_w_up.astype(BF16), "wd": ffn_w_down.astype(BF16),
        "win": win, "wuqt": wuqt, "wuk": wuk, "wuvt": wuvt,
        "wpool": w_pool.astype(BF16), "wout": w_out.astype(BF16),
    }
    y_prompt = _trunk(x_prompt, mod_all[:, :bp], modf_all[:bp], p)
    y_sample = _trunk(x_sample, mod_all[:, bp:], modf_all[bp:], p)
    return (y_prompt, y_sample)
```

```python
import functools
import math

import jax
import jax.numpy as jnp
from jax import lax
from jax.experimental import pallas as pl
from jax.experimental.pallas import tpu as pltpu

F32 = jnp.float32
BF16 = jnp.bfloat16

N_HEADS = 8
QK_NOPE_DIM = 128
QK_ROPE_DIM = 64
V_HEAD_DIM = 128
Q_LORA_RANK = 512
KV_LORA_RANK = 512
ROPE_BASE = 10000.0
POOL_WINDOWS = (2, 4, 8, 16)
POOL_GROUP_WIDTH = 256
POOL_WIDTH = POOL_GROUP_WIDTH * len(POOL_WINDOWS)
ATTN_WIDTH = N_HEADS * V_HEAD_DIM
N_MOD = 9
NORM_EPS = 1e-6
HEAD_QK = QK_NOPE_DIM + 2 * QK_ROPE_DIM
QK_WIDTH = N_HEADS * HEAD_QK
FIRST_STEP_ROWS = 256
HALO = 16
BAND_ROWS = 256
LOG2E = math.log2(math.e)
NT_DIMS = (((1,), (1,)), ((), ()))

VMEM_LIMIT_CAP = 60 * 1024 * 1024


def _vmem_limit(block_bytes, scratch_bytes, temp_bytes):
    return int(min(VMEM_LIMIT_CAP, 2 * block_bytes + scratch_bytes + temp_bytes + (4 << 20)))


def _pick_tile(n, preferred):
    for t in preferred:
        if n % t == 0:
            return t
    return n


def _seq_tile(seq_len):
    return _pick_tile(seq_len, (512, 256, 128))


def _rms(x, gain):
    ms = jnp.mean(x * x, axis=-1, keepdims=True)
    return x * lax.rsqrt(ms + NORM_EPS) * gain


def _mod_kernel(c_ref, w_ref, b_ref, o_ref):
    c = c_ref[...]
    cond = (c * jax.nn.sigmoid(c)).astype(BF16)
    o_ref[...] = jnp.dot(cond, w_ref[...].astype(BF16), preferred_element_type=F32) + b_ref[...]


def _modulation(c, w, b):
    L, D, N = w.shape
    R = c.shape[0]
    tn = _pick_tile(N, (1024, 512, 256, 128))
    return pl.pallas_call(
        _mod_kernel,
        grid=(L, N // tn),
        in_specs=[
            pl.BlockSpec((R, D), lambda l, j: (0, 0)),
            pl.BlockSpec((None, D, tn), lambda l, j: (l, 0, j)),
            pl.BlockSpec((None, 1, tn), lambda l, j: (l, 0, j)),
        ],
        out_specs=pl.BlockSpec((None, R, tn), lambda l, j: (l, 0, j)),
        out_shape=jax.ShapeDtypeStruct((L, R, N), F32),
        compiler_params=pltpu.CompilerParams(
            dimension_semantics=("parallel", "parallel"),
            vmem_limit_bytes=_vmem_limit(D * tn * 4 + R * D * 4 + R * tn * 4, 0, D * tn * 2),
        ),
        name="adaln_mod",
    )(c, w, b.reshape(L, 1, N))


def _ffn_kernel(x_ref, mod_ref, g_ref, wg_ref, wu_ref, wd_ref, modf_ref, gf_ref, o_ref, h_ref, *, mod_base, final):
    j = pl.program_id(1)
    last = pl.num_programs(1) - 1
    piece = min(FIRST_STEP_ROWS, x_ref.shape[0])
    res_gate = 0.5 * mod_ref[0, mod_base + 2:mod_base + 3, :]

    def gated_chunk(h):
        gate = jnp.dot(h, wg_ref[...], preferred_element_type=F32)
        up = jnp.dot(h, wu_ref[...], preferred_element_type=F32)
        act = (gate * jax.nn.sigmoid(gate) * up).astype(BF16)
        return res_gate * jnp.dot(act, wd_ref[...], preferred_element_type=F32)

    @pl.when(j == 0)
    def _():
        shift = mod_ref[0, mod_base:mod_base + 1, :]
        scale = mod_ref[0, mod_base + 1:mod_base + 2, :]
        for r in range(0, x_ref.shape[0], piece):
            x = x_ref[r:r + piece, :]
            h = (_rms(x, g_ref[...]) * (1.0 + scale) + shift).astype(BF16)
            h_ref[r:r + piece, :] = h
            o_ref[r:r + piece, :] = x + gated_chunk(h)

    @pl.when((j > 0) & (j < last) if final else j > 0)
    def _():
        o_ref[...] += gated_chunk(h_ref[...])

    if final:
        @pl.when(j == last)
        def _():
            for r in range(0, x_ref.shape[0], piece):
                y = o_ref[r:r + piece, :] + gated_chunk(h_ref[r:r + piece, :])
                o_ref[r:r + piece, :] = _rms(y, gf_ref[...]) * (1.0 + modf_ref[0, 1:2, :]) + modf_ref[0, 0:1, :]


def _ffn(x, mod, gain, wg, wu, wd, mod_final, gain_final, layer, half, mod_base, seq_len, final):
    T, D = x.shape
    F = wg.shape[-1]
    tm = _pick_tile(seq_len, (1024, 512, 256, 128))
    tf = _pick_tile(F, (512, 256, 128))
    tps = seq_len // tm
    assert F // tf >= 2
    return pl.pallas_call(
        functools.partial(_ffn_kernel, mod_base=mod_base, final=final),
        grid=(T // tm, F // tf),
        in_specs=[
            pl.BlockSpec((tm, D), lambda i, j: (i, 0)),
            pl.BlockSpec((1, N_MOD, D), lambda i, j: (i // tps, 0, 0)),
            pl.BlockSpec((1, D), lambda i, j: (0, 0)),
            pl.BlockSpec((None, None, D, tf), lambda i, j: (layer, half, 0, j)),
            pl.BlockSpec((None, None, D, tf), lambda i, j: (layer, half, 0, j)),
            pl.BlockSpec((None, None, tf, D), lambda i, j: (layer, half, j, 0)),
            pl.BlockSpec((1, 2, D), lambda i, j: (i // tps, 0, 0)),
            pl.BlockSpec((1, D), lambda i, j: (0, 0)),
        ],
        out_specs=pl.BlockSpec((tm, D), lambda i, j: (i, 0)),
        out_shape=jax.ShapeDtypeStruct((T, D), F32),
        scratch_shapes=[pltpu.VMEM((tm, D), BF16)],
        compiler_params=pltpu.CompilerParams(
            dimension_semantics=("parallel", "arbitrary"),
            vmem_limit_bytes=_vmem_limit(
                2 * tm * D * 4 + 3 * D * tf * 2 + 16 * D * 4, tm * D * 2, 3 * tm * tf * 4),
        ),
        name="ffn",
    )(x, mod, gain, wg, wu, wd, mod_final, gain_final)


def _mix_in_kernel(x_ref, mod_ref, g_ref, win_ref, qg_ref, kvg_ref, wuqt_ref, wuk_ref, wuvt_ref, qtab_ref, ktab_ref,
                   q_ref, k_ref, v_ref, u_ref):
    shift = mod_ref[0, 3:4, :]
    scale = mod_ref[0, 4:5, :]
    h = (_rms(x_ref[...], g_ref[...]) * (1.0 + scale) + shift).astype(BF16)
    z = jnp.dot(h, win_ref[...], preferred_element_type=F32)
    c_kv = Q_LORA_RANK
    c_kr = c_kv + KV_LORA_RANK
    c_u = c_kr + 2 * QK_ROPE_DIM
    u_ref[...] = z[:, c_u:]

    qn = _rms(z[:, :c_kv], qg_ref[...]).astype(BF16)
    kvn = _rms(z[:, c_kv:c_kr], kvg_ref[...]).astype(BF16)

    q_t = lax.dot_general(wuqt_ref[...], qn, NT_DIMS, preferred_element_type=F32)
    qtab = qtab_ref[...]
    for hd in range(N_HEADS):
        rows = slice(hd * HEAD_QK, (hd + 1) * HEAD_QK)
        q_ref[0, 0, rows, :] = (q_t[rows, :] * qtab).astype(BF16)
    v_t = lax.dot_general(wuvt_ref[...], kvn, NT_DIMS, preferred_element_type=F32)
    v_ref[0, 0] = v_t.astype(BF16)

    t = z[:, c_kr:c_u] * ktab_ref[...]
    k_rot = (t + pltpu.roll(t, QK_ROPE_DIM, axis=1)).astype(BF16)
    k_nope = jnp.dot(kvn, wuk_ref[...], preferred_element_type=F32)
    for hd in range(N_HEADS):
        c0 = hd * HEAD_QK
        k_ref[:, c0:c0 + QK_NOPE_DIM] = k_nope[:, hd * QK_NOPE_DIM:(hd + 1) * QK_NOPE_DIM].astype(BF16)
        k_ref[:, c0 + QK_NOPE_DIM:c0 + HEAD_QK] = k_rot


def _mix_in(x, mod, gain, win, qg, kvg, wuqt, wuk, wuvt, qtab_t, ktab, layer, batch, seq_len):
    T, D = x.shape
    n_in = win.shape[-1]
    tm = _seq_tile(seq_len)
    tps = seq_len // tm
    whole = lambda shape: pl.BlockSpec(shape, lambda i: (0,) * len(shape))
    layer_mat = lambda r, c: pl.BlockSpec((None, r, c), lambda i: (layer, 0, 0))
    block_bytes = (tm * D * 4 + D * n_in * 2 + Q_LORA_RANK * QK_WIDTH * 2 + KV_LORA_RANK * 2 * ATTN_WIDTH * 2
                   + tm * (HEAD_QK + 2 * QK_ROPE_DIM) * 4 + 2 * tm * QK_WIDTH * 2 + tm * ATTN_WIDTH * 2
                   + tm * POOL_WIDTH * 4 + 16 * D * 4)
    return pl.pallas_call(
        _mix_in_kernel,
        grid=(T // tm,),
        in_specs=[
            pl.BlockSpec((tm, D), lambda i: (i, 0)),
            pl.BlockSpec((1, N_MOD, D), lambda i: (i // tps, 0, 0)),
            whole((1, D)),
            layer_mat(D, n_in),
            whole((1, Q_LORA_RANK)),
            whole((1, KV_LORA_RANK)),
            layer_mat(QK_WIDTH, Q_LORA_RANK),
            layer_mat(KV_LORA_RANK, N_HEADS * QK_NOPE_DIM),
            layer_mat(ATTN_WIDTH, KV_LORA_RANK),
            pl.BlockSpec((HEAD_QK, tm), lambda i: (0, i % tps)),
            pl.BlockSpec((tm, 2 * QK_ROPE_DIM), lambda i: (i % tps, 0)),
        ],
        out_specs=[
            pl.BlockSpec((1, 1, QK_WIDTH, tm), lambda i: (i // tps, i % tps, 0, 0)),
            pl.BlockSpec((tm, QK_WIDTH), lambda i: (i, 0)),
            pl.BlockSpec((1, 1, ATTN_WIDTH, tm), lambda i: (i // tps, i % tps, 0, 0)),
            pl.BlockSpec((tm, POOL_WIDTH), lambda i: (i, 0)),
        ],
        out_shape=[
            jax.ShapeDtypeStruct((batch, tps, QK_WIDTH, tm), BF16),
            jax.ShapeDtypeStruct((T, QK_WIDTH), BF16),
            jax.ShapeDtypeStruct((batch, tps, ATTN_WIDTH, tm), BF16),
            jax.ShapeDtypeStruct((T, POOL_WIDTH), F32),
        ],
        compiler_params=pltpu.CompilerParams(
            dimension_semantics=("parallel",),
            vmem_limit_bytes=_vmem_limit(block_bytes, 0, tm * (n_in + 2 * QK_WIDTH) * 4 + tm * D * 4),
        ),
        name="mix_in",
    )(x, mod, gain, win, qg, kvg, wuqt, wuk, wuvt, qtab_t, ktab)


def _attn_kernel(q_ref, k_ref, v_ref, o_ref, s0_ref, s1_ref):
    n_t, tq = q_ref.shape[1], q_ref.shape[3]
    heads = q_ref.shape[2] // HEAD_QK

    def step(cur, s_cur, prev, s_prev, m_prev):
        if cur is not None:
            qk_cols = slice(cur[0] * HEAD_QK, (cur[0] + 1) * HEAD_QK)
            q_t = q_ref[0, cur[1], qk_cols, :]
        if prev is not None:
            v_rows = slice(prev[0] * V_HEAD_DIM, (prev[0] + 1) * V_HEAD_DIM)
        m = l = acc = None
        for c in range(n_t):
            rows = slice(c * tq, (c + 1) * tq)
            if cur is not None:
                s = jnp.dot(k_ref[0, rows, qk_cols], q_t, preferred_element_type=F32)
                s_cur[rows, :] = s
                mc = jnp.max(s, axis=0, keepdims=True)
                m = mc if m is None else jnp.maximum(m, mc)
            if prev is not None:
                p = jnp.exp2(s_prev[rows, :] - m_prev)
                pv = jnp.dot(v_ref[0, c, v_rows, :], p.astype(BF16), preferred_element_type=F32)
                lc = jnp.sum(p, axis=0, keepdims=True)
                acc = pv if acc is None else acc + pv
                l = lc if l is None else l + lc
        if prev is not None:
            start = prev[1] * tq
            out_rows = pl.ds(start if isinstance(start, int) else pl.multiple_of(start, tq), tq)
            o_ref[0, out_rows, v_rows] = (acc / l).T
        return m

    m, prev = None, None
    for h in range(heads):
        m = step((h, 0), s0_ref, prev, s1_ref, m)

        def pair(i, m, h=h):
            m = step((h, 2 * i + 1), s1_ref, (h, 2 * i), s0_ref, m)
            return step((h, 2 * i + 2), s0_ref, (h, 2 * i + 1), s1_ref, m)

        m = lax.fori_loop(0, (n_t - 1) // 2, pair, m)
        if n_t % 2 == 0:
            m = step((h, n_t - 1), s1_ref, (h, n_t - 2), s0_ref, m)
        prev = (h, n_t - 1)
    step(None, None, prev, s1_ref if n_t % 2 == 0 else s0_ref, m)


def _attention(q_t, k, v_t):
    B, n_t, _, tq = q_t.shape
    S = k.shape[1]
    head_bytes = S * HEAD_QK * 2 * 2 + S * V_HEAD_DIM * 2 + S * V_HEAD_DIM * 4
    scratch_bytes = 2 * S * tq * 4
    temp_bytes = 12 * tq * tq * 4
    heads = 1
    while (n_t % 2 == 0 and heads < N_HEADS
           and 4 * heads * head_bytes + scratch_bytes + temp_bytes <= VMEM_LIMIT_CAP):
        heads *= 2
    return pl.pallas_call(
        _attn_kernel,
        grid=(B, N_HEADS // heads),
        in_specs=[
            pl.BlockSpec((1, n_t, heads * HEAD_QK, tq), lambda b, h: (b, 0, h, 0)),
            pl.BlockSpec((1, S, heads * HEAD_QK), lambda b, h: (b, 0, h)),
            pl.BlockSpec((1, n_t, heads * V_HEAD_DIM, tq), lambda b, h: (b, 0, h, 0)),
        ],
        out_specs=pl.BlockSpec((1, S, heads * V_HEAD_DIM), lambda b, h: (b, 0, h)),
        out_shape=jax.ShapeDtypeStruct((B, S, N_HEADS * V_HEAD_DIM), F32),
        scratch_shapes=[pltpu.VMEM((S, tq), F32), pltpu.VMEM((S, tq), F32)],
        compiler_params=pltpu.CompilerParams(
            dimension_semantics=("parallel", "parallel"),
            vmem_limit_bytes=_vmem_limit(heads * head_bytes, scratch_bytes, temp_bytes),
        ),
        name="attn",
    )(q_t, k, v_t)


def _mix_out_kernel(x_ref, mod_ref, ao_ref, aog_ref, u_ref, up_ref, un_ref, band_ref, wpool_ref, ps_ref, wout_ref,
                    o_ref, ext_ref, cat_ref, *, tps, seq_len):
    tm = x_ref.shape[0]
    si = pl.program_id(0) % tps
    cat_ref[:, :ATTN_WIDTH] = _rms(ao_ref[...], aog_ref[...]).astype(BF16)

    u = u_ref[...]
    ext_ref[0:HALO, :] = jnp.where(si == 0, 0.0, up_ref[...]).astype(BF16)
    ext_ref[HALO:HALO + tm, :] = u.astype(BF16)
    ext_ref[HALO + tm:2 * HALO + tm, :] = jnp.where(si == tps - 1, 0.0, un_ref[...]).astype(BF16)
    ext_ref[2 * HALO + tm:, :] = jnp.zeros((BAND_ROWS - 2 * HALO, POOL_WIDTH), BF16)
    pos = si * tm + lax.broadcasted_iota(jnp.int32, (tm, 1), 0)
    for g, w in enumerate(POOL_WINDOWS):
        c0 = g * POOL_GROUP_WIDTH
        cols = slice(c0, c0 + POOL_GROUP_WIDTH)
        win_sum = jnp.concatenate(
            [jnp.dot(band_ref[g], ext_ref[r:r + 2 * BAND_ROWS, cols], preferred_element_type=F32)
             for r in range(0, tm, BAND_ROWS)], axis=0)
        lo = jnp.maximum(pos - (w // 2 - 1), 0)
        hi = jnp.minimum(pos + w // 2 + 1, seq_len)
        pooled = win_sum / (hi - lo).astype(F32) - u[:, cols]
        y = jnp.dot(pooled.astype(BF16), wpool_ref[g], preferred_element_type=F32) * ps_ref[:, cols]
        cat_ref[:, ATTN_WIDTH + c0:ATTN_WIDTH + c0 + POOL_GROUP_WIDTH] = y.astype(BF16)

    out = jnp.dot(cat_ref[...], wout_ref[...], preferred_element_type=F32)
    o_ref[...] = x_ref[...] + mod_ref[0, 5:6, :] * out


def _pool_bands():
    t = jnp.arange(BAND_ROWS)[:, None]
    s = jnp.arange(2 * BAND_ROWS)[None, :] - HALO
    return jnp.stack([(s >= t - (w // 2 - 1)) & (s <= t + w // 2) for w in POOL_WINDOWS]).astype(BF16)


def _mix_out(x, mod, ao, aog, u, wpool, ps, wout, layer, seq_len):
    T, D = x.shape
    tm = _pick_tile(seq_len, (512, 256))
    tps = seq_len // tm
    assert tm % BAND_ROWS == 0 and max(POOL_WINDOWS) // 2 <= HALO
    halo_blocks = tm // HALO
    n_halo = T // HALO
    n_win = len(POOL_WINDOWS)
    ext_rows = tm + BAND_ROWS
    block_bytes = (2 * tm * D * 4 + tm * ATTN_WIDTH * 4 + (tm + 2 * HALO) * POOL_WIDTH * 4
                   + n_win * BAND_ROWS * 2 * BAND_ROWS * 2 + n_win * POOL_GROUP_WIDTH ** 2 * 2
                   + (ATTN_WIDTH + POOL_WIDTH) * D * 2 + 32 * D * 4)
    return pl.pallas_call(
        functools.partial(_mix_out_kernel, tps=tps, seq_len=seq_len),
        grid=(T // tm,),
        in_specs=[
            pl.BlockSpec((tm, D), lambda i: (i, 0)),
            pl.BlockSpec((1, N_MOD, D), lambda i: (i // tps, 0, 0)),
            pl.BlockSpec((tm, ATTN_WIDTH), lambda i: (i, 0)),
            pl.BlockSpec((1, ATTN_WIDTH), lambda i: (0, 0)),
            pl.BlockSpec((tm, POOL_WIDTH), lambda i: (i, 0)),
            pl.BlockSpec((HALO, POOL_WIDTH), lambda i: (jnp.maximum(i * halo_blocks - 1, 0), 0)),
            pl.BlockSpec((HALO, POOL_WIDTH), lambda i: (jnp.minimum((i + 1) * halo_blocks, n_halo - 1), 0)),
            pl.BlockSpec((n_win, BAND_ROWS, 2 * BAND_ROWS), lambda i: (0, 0, 0)),
            pl.BlockSpec((None, n_win, POOL_GROUP_WIDTH, POOL_GROUP_WIDTH), lambda i: (layer, 0, 0, 0)),
            pl.BlockSpec((1, POOL_WIDTH), lambda i: (0, 0)),
            pl.BlockSpec((None, ATTN_WIDTH + POOL_WIDTH, D), lambda i: (layer, 0, 0)),
        ],
        out_specs=pl.BlockSpec((tm, D), lambda i: (i, 0)),
        out_shape=jax.ShapeDtypeStruct((T, D), F32),
        scratch_shapes=[
            pltpu.VMEM((ext_rows, POOL_WIDTH), BF16),
            pltpu.VMEM((tm, ATTN_WIDTH + POOL_WIDTH), BF16),
        ],
        compiler_params=pltpu.CompilerParams(
            dimension_semantics=("parallel",),
            vmem_limit_bytes=_vmem_limit(block_bytes, ext_rows * POOL_WIDTH * 2 + tm * D * 2, 2 * tm * D * 4),
        ),
        name="mix_out",
    )(x, mod, ao, aog, u, u, u, _pool_bands(), wpool, ps, wout)


def _rot_half_cols(w):
    half = w.shape[-1] // 2
    return jnp.concatenate([-w[..., half:], w[..., :half]], axis=-1)


def _rope_tables(seq_len):
    inv = 1.0 / (ROPE_BASE ** (jnp.arange(0, QK_ROPE_DIM, 2, dtype=F32) / QK_ROPE_DIM))
    ang = jnp.arange(seq_len, dtype=F32)[:, None] * inv[None, :]
    cos, sin = jnp.cos(ang), jnp.sin(ang)
    ktab = jnp.concatenate([cos, cos, sin, sin], axis=-1)
    qscale = (QK_NOPE_DIM + QK_ROPE_DIM) ** -0.5 * LOG2E
    qtab = qscale * jnp.concatenate([jnp.ones((seq_len, QK_NOPE_DIM), F32), ktab], axis=-1)
    return qtab.T, ktab


def _trunk(x, mod, mod_final, p):
    B, S, D = x.shape
    L = p["norm_g"].shape[0]
    assert L >= 1
    qtab_t, ktab = _rope_tables(S)
    ffn = functools.partial(_ffn, wg=p["wg"], wu=p["wu"], wd=p["wd"], mod_final=mod_final,
                            gain_final=p["final_g"][None], seq_len=S)
    xf = x.reshape(B * S, D)
    for l in range(L):
        ml = mod[l]
        xf = ffn(xf, ml, p["norm_g"][l, 0][None], layer=l, half=0, mod_base=0, final=False)
        q_t, k, v_t, u = _mix_in(xf, ml, p["norm_g"][l, 1][None], p["win"], p["q_norm_g"][l][None],
                                 p["kv_norm_g"][l][None], p["wuqt"], p["wuk"], p["wuvt"], qtab_t, ktab, l, B, S)
        ao = _attention(q_t, k.reshape(B, S, QK_WIDTH), v_t)
        xf = _mix_out(xf, ml, ao.reshape(B * S, ATTN_WIDTH), p["attn_out_g"][l][None], u, p["wpool"],
                      p["pool_scale"][l][None], p["wout"], l, S)
        xf = ffn(xf, ml, p["norm_g"][l, 2][None], layer=l, half=1, mod_base=6, final=(l == L - 1))
    return xf.reshape(B, S, D)


def kernel(x_prompt, x_sample, c_prompt, c_sample, norm_g, w_mod, b_mod, ffn_w_gate, ffn_w_up, ffn_w_down, w_in, q_norm_g, kv_norm_g, w_uq, w_ukv, attn_out_g, w_pool, pool_scale, w_out, w_mod_final, b_mod_final, final_g):
    L, D = norm_g.shape[0], norm_g.shape[-1]
    bp, bs = x_prompt.shape[0], x_sample.shape[0]

    c_all = jnp.concatenate([c_prompt, c_sample], axis=0)
    mod_all = _modulation(c_all, w_mod, b_mod).reshape(L, bp + bs, N_MOD, D)
    modf_all = _modulation(c_all, w_mod_final[None], b_mod_final[None]).reshape(bp + bs, 2, D)

    c_kr = Q_LORA_RANK + KV_LORA_RANK
    c_u = c_kr + QK_ROPE_DIM
    w_in = w_in.astype(BF16)
    win = jnp.concatenate([w_in[:, :, :c_u], _rot_half_cols(w_in[:, :, c_kr:c_u]), w_in[:, :, c_u:]], axis=-1)
    wq = w_uq.astype(BF16).reshape(L, Q_LORA_RANK, N_HEADS, QK_NOPE_DIM + QK_ROPE_DIM)
    wq = jnp.concatenate([wq, _rot_half_cols(wq[..., QK_NOPE_DIM:])], axis=-1)
    wuqt = jnp.swapaxes(wq.reshape(L, Q_LORA_RANK, QK_WIDTH), 1, 2)
    wkv = w_ukv.astype(BF16).reshape(L, KV_LORA_RANK, N_HEADS, QK_NOPE_DIM + V_HEAD_DIM)
    wuk = wkv[..., :QK_NOPE_DIM].reshape(L, KV_LORA_RANK, N_HEADS * QK_NOPE_DIM)
    wuvt = jnp.swapaxes(wkv[..., QK_NOPE_DIM:].reshape(L, KV_LORA_RANK, ATTN_WIDTH), 1, 2)
    p = {
        "norm_g": norm_g, "q_norm_g": q_norm_g, "kv_norm_g": kv_norm_g, "attn_out_g": attn_out_g,
        "pool_scale": pool_scale, "final_g": final_g,
        "wg": ffn_w_gate.astype(BF16), "wu": ffn_w_up.astype(BF16), "wd": ffn_w_down.astype(BF16),
        "win": win, "wuqt": wuqt, "wuk": wuk, "wuvt": wuvt,
        "wpool": w_pool.astype(BF16), "wout": w_out.astype(BF16),
    }
    y_prompt = _trunk(x_prompt, mod_all[:, :bp], modf_all[:bp], p)
    y_sample = _trunk(x_sample, mod_all[:, bp:], modf_all[bp:], p)
    return (y_prompt, y_sample)
```
